```python
import jax, jax.numpy as jnp
from jax import lax
import numpy as np

D_MODEL = 1024
BATCH = 1
SEQ = 16384
DEPTH = 4
DEC_BATCH = 8
DEC_SEQ = 4096
PAST_LEN = 128

N_META = 16
EPS = 1e-6
CONV_WIDTH = 3 * D_MODEL // 8
CONV_HEADS = 6
POOL_GROUPS = 4
POOL_WINDOWS = (2, 4, 8, 16)
POOL_WIDTH = 3 * D_MODEL // 8
POOL_GROUP_DIM = POOL_WIDTH // POOL_GROUPS
FOURIER_HEADS = 4
FOURIER_WIDTH = D_MODEL - CONV_WIDTH - POOL_WIDTH
FOURIER_HEAD_DIM = FOURIER_WIDTH // FOURIER_HEADS
MIX_WIDTH = CONV_WIDTH + POOL_WIDTH + FOURIER_WIDTH
IN_WIDTH = 3 * CONV_WIDTH + POOL_WIDTH + FOURIER_WIDTH
D_FF = -(-8 * D_MODEL // (3 * 256)) * 256

kernel_name = "hymba_conv_pool_fourier_encoder"


def _rms(x):
    xf = x.astype(jnp.float32)
    return xf * lax.rsqrt(jnp.mean(xf * xf, axis=-1, keepdims=True) + EPS)


def _rmsnorm(x, g):
    return (_rms(x) * g.astype(jnp.float32)).astype(x.dtype)


def _short_conv(u, w):
    up = jnp.pad(u, ((0, 0), (1, 1), (0, 0)))
    return up[:, :-2] * w[0] + up[:, 1:-1] * w[1] + up[:, 2:] * w[2]


def _centred_mean_minus_self(u, window):
    L = u.shape[1]
    left = window // 2
    right = window - 1 - left
    uf = u.astype(jnp.float32)
    cs = jnp.pad(jnp.cumsum(uf, axis=1), ((0, 0), (1, 0), (0, 0)))
    csp = jnp.pad(cs, ((0, 0), (left, right), (0, 0)), mode="edge")
    hi = csp[:, left + right + 1:left + right + 1 + L]
    lo = csp[:, :L]
    pos = jnp.arange(L)
    cnt = (jnp.minimum(pos + right, L - 1) - jnp.maximum(pos - left, 0) + 1).astype(jnp.float32)
    return ((hi - lo) / cnt[None, :, None] - uf).astype(u.dtype)


def _multiscale_pool(u, pool_w, pool_scale):
    outs = []
    for gi, win in enumerate(POOL_WINDOWS):
        ug = u[..., gi * POOL_GROUP_DIM:(gi + 1) * POOL_GROUP_DIM]
        outs.append(_centred_mean_minus_self(ug, win) @ pool_w[gi])
    return jnp.concatenate(outs, axis=-1) * pool_scale


def _fourier(u):
    B_, L, _ = u.shape
    uh = u.astype(jnp.float32).reshape(B_, L, FOURIER_HEADS, FOURIER_HEAD_DIM).transpose(0, 2, 1, 3)
    f = jnp.fft.fft2(uh, axes=(-2, -1), norm="ortho").real
    return f.transpose(0, 2, 1, 3).reshape(B_, L, FOURIER_WIDTH).astype(u.dtype)


def _layer(h, norm1_g, w_in, conv_w, pool_w, pool_scale, mix_g, w_out, norm2_g, w_gate_up, w_down):
    u = _rmsnorm(h, norm1_g)
    z = u @ w_in
    c0 = CONV_WIDTH
    xa, gb, gc = z[..., :c0], z[..., c0:2 * c0], z[..., 2 * c0:3 * c0]
    xp = z[..., 3 * c0:3 * c0 + POOL_WIDTH]
    xf = z[..., 3 * c0 + POOL_WIDTH:]
    a = gb * _short_conv(gc * xa, conv_w)
    p = _multiscale_pool(xp, pool_w, pool_scale)
    f = _fourier(xf)
    mix = jnp.concatenate([_rms(a), _rms(p), _rms(f)], axis=-1) * mix_g.astype(jnp.float32)
    h = h + mix.astype(h.dtype) @ w_out
    v = _rmsnorm(h, norm2_g)
    gu = v @ w_gate_up
    gate, up = gu[..., :D_FF], gu[..., D_FF:]
    return h + (jax.nn.silu(gate) * up) @ w_down


def _trunk(x, meta_tokens, norm1_g, w_in, conv_w, pool_w, pool_scale, mix_g, w_out, norm2_g, w_gate_up, w_down, final_g):
    B_ = x.shape[0]
    meta = jnp.broadcast_to(meta_tokens.astype(x.dtype)[None], (B_, N_META, D_MODEL))
    h = jnp.concatenate([meta, x], axis=1)
    for l in range(DEPTH):
        h = _layer(h, norm1_g[l], w_in[l], conv_w[l], pool_w[l], pool_scale[l], mix_g[l],
                   w_out[l], norm2_g[l], w_gate_up[l], w_down[l])
    return _rmsnorm(h, final_g)[:, N_META:]


def setup_inputs(seed: int = 0) -> dict:
    key = jax.random.key(seed)
    ks = jax.random.split(key, 16)
    f32 = jnp.float32
    nrm = lambda k, shape, s: jax.random.normal(k, shape, f32) * s
    return {
        "x_prompt": nrm(ks[0], (BATCH, SEQ, D_MODEL), 1.0),
        "x_sample": nrm(ks[1], (DEC_BATCH, DEC_SEQ, D_MODEL), 1.0),
        "meta_tokens": nrm(ks[2], (N_META, D_MODEL), 1.0),
        "norm1_g": 1.0 + nrm(ks[3], (DEPTH, D_MODEL), 0.02),
        "w_in": nrm(ks[4], (DEPTH, D_MODEL, IN_WIDTH), D_MODEL ** -0.5),
        "conv_w": nrm(ks[5], (DEPTH, 3, CONV_WIDTH), 3 ** -0.5),
        "pool_w": nrm(ks[6], (DEPTH, POOL_GROUPS, POOL_GROUP_DIM, POOL_GROUP_DIM), POOL_GROUP_DIM ** -0.5),
        "pool_scale": 1.0 + nrm(ks[7], (DEPTH, POOL_WIDTH), 0.1),
        "mix_g": 1.0 + nrm(ks[8], (DEPTH, MIX_WIDTH), 0.02),
        "w_out": nrm(ks[9], (DEPTH, MIX_WIDTH, D_MODEL), MIX_WIDTH ** -0.5),
        "norm2_g": 1.0 + nrm(ks[10], (DEPTH, D_MODEL), 0.02),
        "w_gate_up": nrm(ks[11], (DEPTH, D_MODEL, 2 * D_FF), D_MODEL ** -0.5),
        "w_down": nrm(ks[12], (DEPTH, D_FF, D_MODEL), D_FF ** -0.5),
        "final_g": 1.0 + nrm(ks[13], (D_MODEL,), 0.02),
    }


def reference(x_prompt, x_sample, meta_tokens, norm1_g, w_in, conv_w, pool_w, pool_scale, mix_g,
              w_out, norm2_g, w_gate_up, w_down, final_g):
    y_prompt = _trunk(x_prompt, meta_tokens, norm1_g, w_in, conv_w, pool_w, pool_scale, mix_g,
                      w_out, norm2_g, w_gate_up, w_down, final_g)
    y_sample = _trunk(x_sample, meta_tokens, norm1_g, w_in, conv_w, pool_w, pool_scale, mix_g,
                      w_out, norm2_g, w_gate_up, w_down, final_g)
    return (y_prompt, y_sample)
```

```python
import functools

import numpy as np
import jax
import jax.numpy as jnp
from jax import lax
from jax.experimental import pallas as pl
from jax.experimental.pallas import tpu as pltpu

D_MODEL = 1024
N_META = 16
EPS = 1e-6
CONV_WIDTH = 384
POOL_WIDTH = 384
POOL_GROUPS = 4
POOL_GROUP_DIM = POOL_WIDTH // POOL_GROUPS
POOL_WINDOWS = (2, 4, 8, 16)
FOURIER_WIDTH = 256
FOURIER_HEAD_DIM = 64
AP_WIDTH = CONV_WIDTH + POOL_WIDTH
IN_WIDTH = 3 * CONV_WIDTH + POOL_WIDTH + FOURIER_WIDTH
D_FF = 2816
FF_CHUNK = 256

HALO = 8
SUBLANES = 8
VMEM_LIMIT_BYTES = 56 * 1024 * 1024

F32 = jnp.float32
BF16 = jnp.bfloat16


def _rms_scale(x):
    return lax.rsqrt(jnp.mean(x * x, axis=-1, keepdims=True) + EPS)


def _in_kernel(hm_ref, hp_ref, hn_ref, g1_ref, win_ref, cw_ref, pbd_ref, ps_ref, mg_ref,
               mixap_ref, xf_ref, *, tile, seq_len):
    n = tile + 2 * HALO
    i = pl.program_id(1)
    hx = jnp.concatenate([hp_ref[0], hm_ref[0], hn_ref[0]], axis=0)
    u = (hx * _rms_scale(hx) * g1_ref[...]).astype(BF16)
    z = jnp.dot(u, win_ref[...], preferred_element_type=F32)
    pos = i * tile - HALO + lax.broadcasted_iota(jnp.int32, (n, 1), 0)
    z = jnp.where((pos >= 0) & (pos < seq_len), z, 0.0)

    c0 = CONV_WIDTH
    xa, gb, gc = z[:, :c0], z[:, c0:2 * c0], z[:, 2 * c0:3 * c0]
    xp = z[:, 3 * c0:3 * c0 + POOL_WIDTH]
    xf = z[:, 3 * c0 + POOL_WIDTH:]

    def shift(x, s):
        return pltpu.roll(x, s % n, axis=0)

    ga = gc * xa
    cw = cw_ref[...]
    conv = shift(ga, 1) * cw[0:1] + ga * cw[1:2] + shift(ga, -1) * cw[2:3]
    a = (gb * conv)[HALO:HALO + tile]

    t2 = xp + shift(xp, 1)
    t4 = t2 + shift(t2, 2)
    t8 = t4 + shift(t4, 4)
    t16 = t8 + shift(t8, 8)
    lane = lax.broadcasted_iota(jnp.int32, (1, POOL_WIDTH), 1)
    grp = [lane < (g + 1) * POOL_GROUP_DIM for g in range(POOL_GROUPS - 1)]
    def by_group(v0, v1, v2, v3):
        return jnp.where(grp[0], v0, jnp.where(grp[1], v1, jnp.where(grp[2], v2, v3)))
    rights = [w - 1 - w // 2 for w in POOL_WINDOWS]
    lefts = [w // 2 for w in POOL_WINDOWS]
    wsum = by_group(t2, shift(t4, -rights[1]), shift(t8, -rights[2]), shift(t16, -rights[3]))
    wsum = wsum[HALO:HALO + tile]
    posm = pos[HALO:HALO + tile]
    left = by_group(*[jnp.full((1, POOL_WIDTH), v, jnp.int32) for v in lefts])
    right = by_group(*[jnp.full((1, POOL_WIDTH), v, jnp.int32) for v in rights])
    cnt = jnp.minimum(posm + right, seq_len - 1) - jnp.maximum(posm - left, 0) + 1
    pm = wsum / cnt.astype(F32) - xp[HALO:HALO + tile]
    p = jnp.dot(pm.astype(BF16), pbd_ref[...], preferred_element_type=F32) * ps_ref[...]

    mg = mg_ref[...]
    an = a * _rms_scale(a) * mg[:, :CONV_WIDTH]
    pn = p * _rms_scale(p) * mg[:, CONV_WIDTH:AP_WIDTH]
    mixap_ref[0] = jnp.concatenate([an, pn], axis=-1).astype(BF16)
    xf_ref[0] = xf[HALO:HALO + tile].astype(BF16)


def _in_call(h, layer, p, *, tile):
    B, L, D = h.shape
    nt = pl.cdiv(L, tile)
    tb = tile // SUBLANES
    last8 = L // SUBLANES - 1
    const = lambda b, i: (layer, 0, 0)
    return pl.pallas_call(
        functools.partial(_in_kernel, tile=tile, seq_len=L),
        grid=(B, nt),
        in_specs=[
            pl.BlockSpec((1, tile, D), lambda b, i: (b, i, 0)),
            pl.BlockSpec((1, HALO, D), lambda b, i: (b, jnp.maximum(i * tb - 1, 0), 0)),
            pl.BlockSpec((1, HALO, D), lambda b, i: (b, jnp.minimum((i + 1) * tb, last8), 0)),
            pl.BlockSpec((None, 1, D), const),
            pl.BlockSpec((None, D, IN_WIDTH), const),
            pl.BlockSpec((None, 3, CONV_WIDTH), const),
            pl.BlockSpec((None, POOL_WIDTH, POOL_WIDTH), const),
            pl.BlockSpec((None, 1, POOL_WIDTH), const),
            pl.BlockSpec((None, 1, D), const),
        ],
        out_specs=[
            pl.BlockSpec((1, tile, AP_WIDTH), lambda b, i: (b, i, 0)),
            pl.BlockSpec((1, tile, FOURIER_WIDTH), lambda b, i: (b, i, 0)),
        ],
        out_shape=[
            jax.ShapeDtypeStruct((B, L, AP_WIDTH), BF16),
            jax.ShapeDtypeStruct((B, L, FOURIER_WIDTH), BF16),
        ],
        compiler_params=pltpu.CompilerParams(
            dimension_semantics=("parallel", "parallel"), vmem_limit_bytes=VMEM_LIMIT_BYTES),
        name="in_proj_mixers",
    )(h, h, h, p["norm1_g"], p["w_in"], p["conv_w"], p["pool_bd"], p["pool_scale"], p["mix_g"])


def _dft1_kernel(x_ref, m1_ref, o_ref, *, m):
    for j in range(m):
        xj = x_ref[0, :, j * FOURIER_WIDTH:(j + 1) * FOURIER_WIDTH]
        o_ref[0, j] = jnp.dot(m1_ref[j], xj, preferred_element_type=F32).astype(BF16)


def _dft3_kernel(gr_ref, gi_ref, c2_ref, s2_ref, c64_ref, s64_ref, o_ref, *, mk, scale):
    c2, s2 = c2_ref[...], s2_ref[...]
    c64, s64 = c64_ref[...], s64_ref[...]
    for j in range(mk):
        cols = slice(j * FOURIER_WIDTH, (j + 1) * FOURIER_WIDTH)
        gr, gi = gr_ref[0, :, cols], gi_ref[0, :, cols]
        yr = jnp.dot(c2, gr, preferred_element_type=F32) + jnp.dot(s2, gi, preferred_element_type=F32)
        yi = jnp.dot(c2, gi, preferred_element_type=F32) - jnp.dot(s2, gr, preferred_element_type=F32)
        f = (jnp.dot(yr.astype(BF16), c64, preferred_element_type=F32)
             + jnp.dot(yi.astype(BF16), s64, preferred_element_type=F32))
        o_ref[0, :, cols] = f * scale


def _angle(idx, period):
    return (idx % period).astype(F32) * (2.0 * np.pi / period)


def _dft_tables(n1, n2):
    n = n1 * n2
    shape = (n2, n1, n1)
    l2, k1, l1 = (lax.broadcasted_iota(jnp.int32, shape, d) for d in range(3))
    ang = _angle(k1 * (n2 * l1 + l2), n)
    m1 = jnp.concatenate([jnp.cos(ang), -jnp.sin(ang)], axis=1)
    k2, j2 = (lax.broadcasted_iota(jnp.int32, (n2, n2), d) for d in range(2))
    ang2 = _angle(k2 * j2, n2)
    d, e = (lax.broadcasted_iota(jnp.int32, (FOURIER_WIDTH, FOURIER_WIDTH), d) for d in range(2))
    same_head = (d // FOURIER_HEAD_DIM) == (e // FOURIER_HEAD_DIM)
    ang64 = _angle(d * e, FOURIER_HEAD_DIM)
    return dict(m1=m1.astype(BF16), c2=jnp.cos(ang2).astype(BF16), s2=jnp.sin(ang2).astype(BF16),
                c64=jnp.where(same_head, jnp.cos(ang64), 0.0).astype(BF16),
                s64=jnp.where(same_head, jnp.sin(ang64), 0.0).astype(BF16))


def _dft_call(xf, tabs, *, n1, n2, m, mk):
    B, L, W = xf.shape
    x2d = xf.reshape(B, n1, n2 * W)
    g = pl.pallas_call(
        functools.partial(_dft1_kernel, m=m),
        grid=(B, pl.cdiv(n2, m)),
        in_specs=[
            pl.BlockSpec((1, n1, W * m), lambda b, i: (b, 0, i)),
            pl.BlockSpec((m, 2 * n1, n1), lambda b, i: (i, 0, 0)),
        ],
        out_specs=pl.BlockSpec((1, m, 2 * n1, W), lambda b, i: (b, i, 0, 0)),
        out_shape=jax.ShapeDtypeStruct((B, n2, 2 * n1, W), BF16),
        compiler_params=pltpu.CompilerParams(
            dimension_semantics=("parallel", "parallel"), vmem_limit_bytes=VMEM_LIMIT_BYTES),
        name="seq_dft_stage1",
    )(x2d, tabs["m1"])
    g2d = g.reshape(B, n2, 2 * n1 * W)
    nk = n1 // mk
    const = lambda b, i: (0, 0)
    f = pl.pallas_call(
        functools.partial(_dft3_kernel, mk=mk, scale=float(1.0 / np.sqrt(L * FOURIER_HEAD_DIM))),
        grid=(B, nk),
        in_specs=[
            pl.BlockSpec((1, n2, W * mk), lambda b, i: (b, 0, i)),
            pl.BlockSpec((1, n2, W * mk), lambda b, i: (b, 0, nk + i)),
            pl.BlockSpec((n2, n2), const),
            pl.BlockSpec((n2, n2), const),
            pl.BlockSpec((W, W), const),
            pl.BlockSpec((W, W), const),
        ],
        out_specs=pl.BlockSpec((1, n2, W * mk), lambda b, i: (b, 0, i)),
        out_shape=jax.ShapeDtypeStruct((B, n2, n1 * W), F32),
        compiler_params=pltpu.CompilerParams(
            dimension_semantics=("parallel", "parallel"), vmem_limit_bytes=VMEM_LIMIT_BYTES),
        name="seq_dft_stage2",
    )(g2d, g2d, tabs["c2"], tabs["s2"], tabs["c64"], tabs["s64"])
    return f.reshape(B, L, W)


def _out_kernel(h_ref, mixap_ref, f_ref, mgf_ref, wout_ref, g2_ref, wgu_ref, wdn_ref, fg_ref,
                o_ref, act_ref, *, last):
    f = f_ref[0]
    fn = (f * _rms_scale(f) * mgf_ref[...]).astype(BF16)
    mix = jnp.concatenate([mixap_ref[0], fn], axis=-1)
    h1 = h_ref[0] + jnp.dot(mix, wout_ref[...], preferred_element_type=F32)
    v = (h1 * _rms_scale(h1) * g2_ref[...]).astype(BF16)
    for c in range(D_FF // FF_CHUNK):
        lo = c * FF_CHUNK
        gate = jnp.dot(v, wgu_ref[:, lo:lo + FF_CHUNK], preferred_element_type=F32)
        up = jnp.dot(v, wgu_ref[:, D_FF + lo:D_FF + lo + FF_CHUNK], preferred_element_type=F32)
        act_ref[:, lo:lo + FF_CHUNK] = (gate * jax.nn.sigmoid(gate) * up).astype(BF16)
    h2 = h1 + jnp.dot(act_ref[...], wdn_ref[...], preferred_element_type=F32)
    if last:
        h2 = h2 * _rms_scale(h2) * fg_ref[...]
    o_ref[0] = h2


def _out_call(h, mixap, f, layer, p, *, tile, last):
    B, L, D = h.shape
    nt = pl.cdiv(L, tile)
    row = lambda b, i: (b, i, 0)
    const = lambda b, i: (layer, 0, 0)
    single = pl.Buffered(1)
    return pl.pallas_call(
        functools.partial(_out_kernel, last=last),
        grid=(B, nt),
        in_specs=[
            pl.BlockSpec((1, tile, D), row),
            pl.BlockSpec((1, tile, AP_WIDTH), row),
            pl.BlockSpec((1, tile, FOURIER_WIDTH), row),
            pl.BlockSpec((None, 1, FOURIER_WIDTH), const),
            pl.BlockSpec((None, D, D), const, pipeline_mode=single),
            pl.BlockSpec((None, 1, D), const),
            pl.BlockSpec((None, D, 2 * D_FF), const, pipeline_mode=single),
            pl.BlockSpec((None, D_FF, D), const, pipeline_mode=single),
            pl.BlockSpec((1, D), lambda b, i: (0, 0)),
        ],
        out_specs=pl.BlockSpec((1, tile, D), row),
        out_shape=jax.ShapeDtypeStruct((B, L, D), F32),
        scratch_shapes=[pltpu.VMEM((tile, D_FF), BF16)],
        compiler_params=pltpu.CompilerParams(
            dimension_semantics=("parallel", "parallel"), vmem_limit_bytes=VMEM_LIMIT_BYTES),
        name="out_proj_ffn",
    )(h, mixap, f, p["mix_g_f"], p["w_out"], p["norm2_g"], p["w_gate_up"], p["w_down"], p["final_g"])


def _trunk(x, meta_tokens, p, *, tile, n1, n2, m, mk):
    B, S, D = x.shape
    assert n1 * n2 == S + N_META and n1 % mk == 0 and tile % 16 == 0
    meta = jnp.broadcast_to(meta_tokens.astype(x.dtype)[None], (B, N_META, D))
    h = jnp.concatenate([meta, x], axis=1)
    tabs = _dft_tables(n1, n2)
    depth = p["w_in"].shape[0]
    for layer in range(depth):
        mixap, xf = _in_call(h, layer, p, tile=tile)
        f = _dft_call(xf, tabs, n1=n1, n2=n2, m=m, mk=mk)
        h = _out_call(h, mixap, f, layer, p, tile=tile, last=(layer == depth - 1))
    return h[:, N_META:]


def kernel(x_prompt, x_sample, meta_tokens, norm1_g, w_in, conv_w, pool_w, pool_scale, mix_g,
           w_out, norm2_g, w_gate_up, w_down, final_g):
    depth = w_in.shape[0]
    pool_bd = jnp.zeros((depth, POOL_WIDTH, POOL_WIDTH), F32)
    for g in range(POOL_GROUPS):
        sl = slice(g * POOL_GROUP_DIM, (g + 1) * POOL_GROUP_DIM)
        pool_bd = pool_bd.at[:, sl, sl].set(pool_w[:, g])
    p = dict(
        norm1_g=norm1_g[:, None, :],
        w_in=w_in.astype(BF16),
        conv_w=conv_w,
        pool_bd=pool_bd.astype(BF16),
        pool_scale=pool_scale[:, None, :],
        mix_g=mix_g[:, None, :],
        mix_g_f=mix_g[:, None, AP_WIDTH:],
        w_out=w_out.astype(BF16),
        norm2_g=norm2_g[:, None, :],
        w_gate_up=w_gate_up.astype(BF16),
        w_down=w_down.astype(BF16),
        final_g=final_g[None, :],
    )
    y_prompt = _trunk(x_prompt, meta_tokens, p, tile=656, n1=200, n2=82, m=2, mk=8)
    y_sample = _trunk(x_sample, meta_tokens, p, tile=688, n1=16, n2=257, m=43, mk=16)
    return (y_prompt, y_sample)
```

```python
import functools

import numpy as np
import jax
import jax.numpy as jnp
from jax import lax
from jax.experimental import pallas as pl
from jax.experimental.pallas import tpu as pltpu

D_MODEL = 1024
N_META = 16
EPS = 1e-6
CONV_WIDTH = 384
POOL_WIDTH = 384
POOL_GROUPS = 4
POOL_GROUP_DIM = POOL_WIDTH // POOL_GROUPS
POOL_WINDOWS = (2, 4, 8, 16)
FOURIER_WIDTH = 256
FOURIER_HEAD_DIM = 64
AP_WIDTH = CONV_WIDTH + POOL_WIDTH
IN_WIDTH = 3 * CONV_WIDTH + POOL_WIDTH + FOURIER_WIDTH
D_FF = 2816
FF_CHUNK = 256

HALO = 8
SUBLANES = 8
LANES = 128
BF16_ROWS = 16
VMEM_LIMIT_BYTES = 56 * 1024 * 1024

F32 = jnp.float32
BF16 = jnp.bfloat16


def _rms_scale(x):
    return lax.rsqrt(jnp.mean(x * x, axis=-1, keepdims=True) + EPS)


def _dot(a, b):
    return jnp.dot(a, b, preferred_element_type=F32)


def _in_kernel(hm_ref, hp_ref, hn_ref, g1_ref, win_ref, cw_ref, pbd_ref, ps_ref, mg_ref,
               mixap_ref, xf_ref, *, tile, seq_len, n_tiles):
    n = tile + 2 * HALO
    main = slice(HALO, HALO + tile)
    i = pl.program_id(1)
    hx = jnp.concatenate([hp_ref[0], hm_ref[0], hn_ref[0]], axis=0)
    u = (hx * _rms_scale(hx) * g1_ref[...]).astype(BF16)
    z = _dot(u, win_ref[...])
    c0 = CONV_WIDTH
    gb = z[main, c0:2 * c0]
    xf_ref[0] = z[main, 3 * c0 + POOL_WIDTH:]

    def shift(x, s):
        return pltpu.roll(x, s % n, axis=0)

    lefts = [w // 2 for w in POOL_WINDOWS]
    rights = [w - 1 - w // 2 for w in POOL_WINDOWS]
    lane384 = lax.broadcasted_iota(jnp.int32, (1, POOL_WIDTH), 1)
    lane128 = lax.broadcasted_iota(jnp.int32, (1, LANES), 1)

    def per_group(vals, dtype):
        out = jnp.full((1, POOL_WIDTH), vals[-1], dtype)
        for g in range(POOL_GROUPS - 2, -1, -1):
            out = jnp.where(lane384 < (g + 1) * POOL_GROUP_DIM, jnp.asarray(vals[g], dtype), out)
        return out

    def mixers(edge):
        ga = z[:, 2 * c0:3 * c0] * z[:, :c0]
        xp = z[:, 3 * c0:3 * c0 + POOL_WIDTH]
        if edge:
            pos = i * tile - HALO + lax.broadcasted_iota(jnp.int32, (n, 1), 0)
            valid = (pos >= 0) & (pos < seq_len)
            ga = jnp.where(valid, ga, 0.0)
            xp = jnp.where(valid, xp, 0.0)
        cw = cw_ref[...]
        conv = shift(ga, 1)[main] * cw[0:1] + ga[main] * cw[1:2] + shift(ga, -1)[main] * cw[2:3]
        a = gb * conv
        cols = []
        for c in range(POOL_WIDTH // LANES):
            x = xp[:, c * LANES:(c + 1) * LANES]
            t2 = x + shift(x, 1)
            t4 = t2 + shift(t2, 2)
            if c == 0:
                lo, hi = t2, shift(t4, -rights[1])
            else:
                t8 = t4 + shift(t4, 4)
                if c == 1:
                    lo, hi = shift(t4, -rights[1]), shift(t8, -rights[2])
                else:
                    t8d = shift(t8, 1)
                    lo, hi = shift(t8, -rights[2]), t8d + shift(t8d, -8)
            boundary = (c + 1) * POOL_GROUP_DIM - c * LANES
            cols.append(jnp.where(lane128 < boundary, lo[main], hi[main]))
        wsum = jnp.concatenate(cols, axis=-1)
        if edge:
            posm = i * tile + lax.broadcasted_iota(jnp.int32, (tile, 1), 0)
            left, right = per_group(lefts, jnp.int32), per_group(rights, jnp.int32)
            cnt = jnp.minimum(posm + right, seq_len - 1) - jnp.maximum(posm - left, 0) + 1
            pm = wsum / cnt.astype(F32) - xp[main]
        else:
            pm = wsum * per_group([1.0 / w for w in POOL_WINDOWS], F32) - xp[main]
        return a, pm

    is_edge = (i == 0) | (i == n_tiles - 1)
    a, pm = lax.cond(is_edge, lambda: mixers(True), lambda: mixers(False))
    p = _dot(pm.astype(BF16), pbd_ref[...]) * ps_ref[...]

    mg = mg_ref[...]
    an = a * _rms_scale(a) * mg[:, :CONV_WIDTH]
    pn = p * _rms_scale(p) * mg[:, CONV_WIDTH:AP_WIDTH]
    mixap_ref[0] = jnp.concatenate([an, pn], axis=-1).astype(BF16)


def _in_call(h, layer, p, *, tile):
    B, L, D = h.shape
    nt = pl.cdiv(L, tile)
    assert nt >= 2 and (nt - 1) * tile + HALO <= L and tile >= HALO
    tb = tile // SUBLANES
    last8 = L // SUBLANES - 1
    const = lambda b, i: (layer, 0, 0)
    return pl.pallas_call(
        functools.partial(_in_kernel, tile=tile, seq_len=L, n_tiles=nt),
        grid=(B, nt),
        in_specs=[
            pl.BlockSpec((1, tile, D), lambda b, i: (b, i, 0)),
            pl.BlockSpec((1, HALO, D), lambda b, i: (b, jnp.maximum(i * tb - 1, 0), 0)),
            pl.BlockSpec((1, HALO, D), lambda b, i: (b, jnp.minimum((i + 1) * tb, last8), 0)),
            pl.BlockSpec((None, 1, D), const),
            pl.BlockSpec((None, D, IN_WIDTH), const),
            pl.BlockSpec((None, 3, CONV_WIDTH), const),
            pl.BlockSpec((None, POOL_WIDTH, POOL_WIDTH), const),
            pl.BlockSpec((None, 1, POOL_WIDTH), const),
            pl.BlockSpec((None, 1, D), const),
        ],
        out_specs=[
            pl.BlockSpec((1, tile, AP_WIDTH), lambda b, i: (b, i, 0)),
            pl.BlockSpec((1, tile, FOURIER_WIDTH), lambda b, i: (b, i, 0)),
        ],
        out_shape=[
            jax.ShapeDtypeStruct((B, L, AP_WIDTH), BF16),
            jax.ShapeDtypeStruct((B, L, FOURIER_WIDTH), F32),
        ],
        compiler_params=pltpu.CompilerParams(
            dimension_semantics=("parallel", "parallel"), vmem_limit_bytes=VMEM_LIMIT_BYTES),
        name="in_proj_mixers",
    )(h, h, h, p["norm1_g"], p["w_in"], p["conv_w"], p["pool_bd"], p["pool_scale"], p["mix_g"])


def _dft_a_kernel(x_ref, ma_ref, o_ref, *, n2, n2p):
    for j in range(SUBLANES):
        xj = x_ref[0, :, j, :].astype(BF16)
        hj = _dot(ma_ref[j], xj)
        o_ref[0, 0, :, j, :] = hj[:n2]
        o_ref[0, 1, :, j, :] = hj[n2p:n2p + n2]


def _dft_b_kernel(h_ref, wb_ref, c64_ref, s64_ref, o_ref, *, rows, chunk, scale):
    c64, s64 = c64_ref[...], s64_ref[...]
    for r0 in range(0, rows, chunk):
        m = min(chunk, rows - r0)
        wc, ws = wb_ref[0, :m, :m], wb_ref[1, :m, :m]
        hr = h_ref[0, 0, r0:r0 + m, :].astype(BF16)
        hi = h_ref[0, 1, r0:r0 + m, :].astype(BF16)
        yr = _dot(wc, hr) + _dot(ws, hi)
        yi = _dot(wc, hi) - _dot(ws, hr)
        f = _dot(yr.astype(BF16), c64) + _dot(yi.astype(BF16), s64)
        o_ref[0, r0:r0 + m, :] = f * scale


def _unpermute_kernel(x_ref, o_ref, *, n2):
    for j in range(SUBLANES):
        o_ref[0, j * n2:(j + 1) * n2, :] = x_ref[0, :, j, :]


def _angle(idx, period):
    return (idx % period).astype(F32) * (2.0 * np.pi / period)


def _dft_tables(n1, n2, groups):
    n = n1 * n2
    n2p = -(-n2 // BF16_ROWS) * BF16_ROWS
    l1, k2, l2 = (lax.broadcasted_iota(jnp.int32, (n1, n2p, n2), d) for d in range(3))
    ang = _angle(k2 * (l1 + n1 * l2), n)
    ma = jnp.concatenate([jnp.cos(ang), -jnp.sin(ang)], axis=1).astype(BF16)
    g = groups * n1
    r, c = (lax.broadcasted_iota(jnp.int32, (g, g), d) for d in range(2))
    angb = _angle((r % n1) * (c % n1), n1)
    same = (r // n1) == (c // n1)
    wb = jnp.stack([jnp.where(same, jnp.cos(angb), 0.0), jnp.where(same, jnp.sin(angb), 0.0)]).astype(BF16)
    d, e = (lax.broadcasted_iota(jnp.int32, (FOURIER_WIDTH, FOURIER_WIDTH), d) for d in range(2))
    same_head = (d // FOURIER_HEAD_DIM) == (e // FOURIER_HEAD_DIM)
    ang64 = _angle(d * e, FOURIER_HEAD_DIM)
    return dict(ma=ma, wb=wb,
                c64=jnp.where(same_head, jnp.cos(ang64), 0.0).astype(BF16),
                s64=jnp.where(same_head, jnp.sin(ang64), 0.0).astype(BF16))


def _dft_call(xf, tabs, *, n1, n2, groups, rows_b):
    B, L, W = xf.shape
    n2p = tabs["ma"].shape[1] // 2
    chunk = groups * n1
    params = pltpu.CompilerParams(
        dimension_semantics=("parallel", "parallel"), vmem_limit_bytes=VMEM_LIMIT_BYTES)
    nblk = n1 // SUBLANES
    h = pl.pallas_call(
        functools.partial(_dft_a_kernel, n2=n2, n2p=n2p),
        grid=(B, nblk),
        in_specs=[
            pl.BlockSpec((1, n2, SUBLANES, W), lambda b, i: (b, 0, i, 0)),
            pl.BlockSpec((SUBLANES, 2 * n2p, n2), lambda b, i: (i, 0, 0)),
        ],
        out_specs=pl.BlockSpec((1, 2, n2, SUBLANES, W), lambda b, i: (b, 0, 0, i, 0)),
        out_shape=jax.ShapeDtypeStruct((B, 2, n2, n1, W), F32),
        compiler_params=params,
        name="seq_dft_stage_a",
    )(xf.reshape(B, n2, n1, W), tabs["ma"])
    const2 = lambda b, i: (0, 0)
    f = pl.pallas_call(
        functools.partial(_dft_b_kernel, rows=rows_b, chunk=chunk,
                          scale=float(1.0 / np.sqrt(L * FOURIER_HEAD_DIM))),
        grid=(B, pl.cdiv(L, rows_b)),
        in_specs=[
            pl.BlockSpec((1, 2, rows_b, W), lambda b, i: (b, 0, i, 0)),
            pl.BlockSpec((2, chunk, chunk), lambda b, i: (0, 0, 0)),
            pl.BlockSpec((W, W), const2),
            pl.BlockSpec((W, W), const2),
        ],
        out_specs=pl.BlockSpec((1, rows_b, W), lambda b, i: (b, i, 0)),
        out_shape=jax.ShapeDtypeStruct((B, L, W), F32),
        compiler_params=params,
        name="seq_dft_stage_b",
    )(h.reshape(B, 2, L, W), tabs["wb"], tabs["c64"], tabs["s64"])
    return pl.pallas_call(
        functools.partial(_unpermute_kernel, n2=n2),
        grid=(B, nblk),
        in_specs=[pl.BlockSpec((1, n2, SUBLANES, W), lambda b, i: (b, 0, i, 0))],
        out_specs=pl.BlockSpec((1, SUBLANES * n2, W), lambda b, i: (b, i, 0)),
        out_shape=jax.ShapeDtypeStruct((B, L, W), F32),
        compiler_params=params,
        name="seq_dft_unpermute",
    )(f.reshape(B, n2, n1, W))


def _out_kernel(h_ref, mixap_ref, f_ref, mgf_ref, wout_ref, g2_ref, wgu_ref, wdn_ref, fg_ref,
                o_ref, act_ref, *, last):
    f = f_ref[0]
    fn = (f * _rms_scale(f) * mgf_ref[...]).astype(BF16)
    mix = jnp.concatenate([mixap_ref[0], fn], axis=-1)
    h1 = h_ref[0] + _dot(mix, wout_ref[...])
    v = (h1 * _rms_scale(h1) * g2_ref[...]).astype(BF16)
    for c in range(D_FF // FF_CHUNK):
        lo = c * FF_CHUNK
        gate = _dot(v, wgu_ref[:, lo:lo + FF_CHUNK])
        up = _dot(v, wgu_ref[:, D_FF + lo:D_FF + lo + FF_CHUNK])
        act_ref[:, lo:lo + FF_CHUNK] = (gate * jax.nn.sigmoid(gate) * up).astype(BF16)
    h2 = h1 + _dot(act_ref[...], wdn_ref[...])
    if last:
        h2 = h2 * _rms_scale(h2) * fg_ref[...]
    o_ref[0] = h2


def _out_call(h, mixap, f, layer, p, *, tile, last):
    B, L, D = h.shape
    nt = pl.cdiv(L, tile)
    row = lambda b, i: (b, i, 0)
    const = lambda b, i: (layer, 0, 0)
    single = pl.Buffered(1)
    return pl.pallas_call(
        functools.partial(_out_kernel, last=last),
        grid=(B, nt),
        in_specs=[
            pl.BlockSpec((1, tile, D), row),
            pl.BlockSpec((1, tile, AP_WIDTH), row),
            pl.BlockSpec((1, tile, FOURIER_WIDTH), row),
            pl.BlockSpec((None, 1, FOURIER_WIDTH), const),
            pl.BlockSpec((None, D, D), const, pipeline_mode=single),
            pl.BlockSpec((None, 1, D), const),
            pl.BlockSpec((None, D, 2 * D_FF), const, pipeline_mode=single),
            pl.BlockSpec((None, D_FF, D), const, pipeline_mode=single),
            pl.BlockSpec((1, D), lambda b, i: (0, 0)),
        ],
        out_specs=pl.BlockSpec((1, tile, D), row),
        out_shape=jax.ShapeDtypeStruct((B, L, D), F32),
        scratch_shapes=[pltpu.VMEM((tile, D_FF), BF16)],
        compiler_params=pltpu.CompilerParams(
            dimension_semantics=("parallel", "parallel"), vmem_limit_bytes=VMEM_LIMIT_BYTES),
        name="out_proj_ffn",
    )(h, mixap, f, p["mix_g_f"], p["w_out"], p["norm2_g"], p["w_gate_up"], p["w_down"], p["final_g"])


def _trunk(x, meta_tokens, p, *, tile, n1, n2, groups, rows_b):
    B, S, D = x.shape
    L = S + N_META
    assert n1 * n2 == L and n1 % SUBLANES == 0 and tile % BF16_ROWS == 0
    assert rows_b % (groups * n1) in (0, L % (groups * n1))
    meta = jnp.broadcast_to(meta_tokens.astype(x.dtype)[None], (B, N_META, D))
    h = jnp.concatenate([meta, x], axis=1)
    tabs = _dft_tables(n1, n2, groups)
    depth = p["w_in"].shape[0]
    for layer in range(depth):
        mixap, xf = _in_call(h, layer, p, tile=tile)
        f = _dft_call(xf, tabs, n1=n1, n2=n2, groups=groups, rows_b=rows_b)
        h = _out_call(h, mixap, f, layer, p, tile=tile, last=(layer == depth - 1))
    return h[:, N_META:]


def kernel(x_prompt, x_sample, meta_tokens, norm1_g, w_in, conv_w, pool_w, pool_scale, mix_g,
           w_out, norm2_g, w_gate_up, w_down, final_g):
    depth = w_in.shape[0]
    pool_bd = jnp.zeros((depth, POOL_WIDTH, POOL_WIDTH), F32)
    for g in range(POOL_GROUPS):
        sl = slice(g * POOL_GROUP_DIM, (g + 1) * POOL_GROUP_DIM)
        pool_bd = pool_bd.at[:, sl, sl].set(pool_w[:, g])
    p = dict(
        norm1_g=norm1_g[:, None, :],
        w_in=w_in.astype(BF16),
        conv_w=conv_w,
        pool_bd=pool_bd.astype(BF16),
        pool_scale=pool_scale[:, None, :],
        mix_g=mix_g[:, None, :],
        mix_g_f=mix_g[:, None, AP_WIDTH:],
        w_out=w_out.astype(BF16),
        norm2_g=norm2_g[:, None, :],
        w_gate_up=w_gate_up.astype(BF16),
        w_down=w_down.astype(BF16),
        final_g=final_g[None, :],
    )
    y_prompt = _trunk(x_prompt, meta_tokens, p, tile=656, n1=200, n2=82, groups=1, rows_b=2000)
    y_sample = _trunk(x_sample, meta_tokens, p, tile=688, n1=16, n2=257, groups=16, rows_b=4112)
    return (y_prompt, y_sample)
```

```python
import functools

import numpy as np
import jax
import jax.numpy as jnp
from jax import lax
from jax.experimental import pallas as pl
from jax.experimental.pallas import tpu as pltpu

D_MODEL = 1024
N_META = 16
EPS = 1e-6
CONV_WIDTH = 384
POOL_WIDTH = 384
POOL_GROUPS = 4
POOL_GROUP_DIM = POOL_WIDTH // POOL_GROUPS
POOL_WINDOWS = (2, 4, 8, 16)
FOURIER_WIDTH = 256
FOURIER_HEAD_DIM = 64
AP_WIDTH = CONV_WIDTH + POOL_WIDTH
IN_WIDTH = 3 * CONV_WIDTH + POOL_WIDTH + FOURIER_WIDTH
D_FF = 2816
FF_CHUNK = 256

HALO = 8
SUBLANES = 8
LANES = 128
BF16_ROWS = 16
VMEM_LIMIT_BYTES = 56 * 1024 * 1024

F32 = jnp.float32
BF16 = jnp.bfloat16


def _rms_scale(x):
    return lax.rsqrt(jnp.mean(x * x, axis=-1, keepdims=True) + EPS)


def _dot(a, b):
    return jnp.dot(a, b, preferred_element_type=F32)


def _in_kernel(hm_ref, hp_ref, hn_ref, g1_ref, win_ref, cw_ref, pbd_ref, ps_ref, mg_ref, icnt_ref,
               mixap_ref, xf_ref, *, tile, seq_len):
    n = tile + 2 * HALO
    main = slice(HALO, HALO + tile)
    i = pl.program_id(1)
    hx = jnp.concatenate([hp_ref[0], hm_ref[0], hn_ref[0]], axis=0)
    u = (hx * _rms_scale(hx) * g1_ref[...]).astype(BF16)
    z = _dot(u, win_ref[...])
    c0 = CONV_WIDTH
    xf_ref[0] = z[main, 3 * c0 + POOL_WIDTH:]
    pos = i * tile - HALO + lax.broadcasted_iota(jnp.int32, (n, 1), 0)
    valid = (pos >= 0) & (pos < seq_len)
    ga = jnp.where(valid, z[:, 2 * c0:3 * c0] * z[:, :c0], 0.0)
    xp = jnp.where(valid, z[:, 3 * c0:3 * c0 + POOL_WIDTH], 0.0)

    def shift(x, s):
        return pltpu.roll(x, s % n, axis=0)

    cw = cw_ref[...]
    conv = shift(ga, 1)[main] * cw[0:1] + ga[main] * cw[1:2] + shift(ga, -1)[main] * cw[2:3]
    a = z[main, c0:2 * c0] * conv

    rights = [w - 1 - w // 2 for w in POOL_WINDOWS]
    lane = lax.broadcasted_iota(jnp.int32, (1, LANES), 1)
    cols = []
    for c in range(POOL_WIDTH // LANES):
        x = xp[:, c * LANES:(c + 1) * LANES]
        t2 = x + shift(x, 1)
        t4 = t2 + shift(t2, 2)
        if c == 0:
            lo, hi = t2, shift(t4, -rights[1])
        else:
            t8 = t4 + shift(t4, 4)
            if c == 1:
                lo, hi = shift(t4, -rights[1]), shift(t8, -rights[2])
            else:
                t8d = shift(t8, 1)
                lo, hi = shift(t8, -rights[2]), t8d + shift(t8d, -8)
        boundary = (c + 1) * POOL_GROUP_DIM - c * LANES
        cols.append(jnp.where(lane < boundary, lo[main], hi[main]))
    wsum = jnp.concatenate(cols, axis=-1)
    pm = wsum * icnt_ref[0] - xp[main]
    p = _dot(pm.astype(BF16), pbd_ref[...]) * ps_ref[...]

    mg = mg_ref[...]
    an = a * _rms_scale(a) * mg[:, :CONV_WIDTH]
    pn = p * _rms_scale(p) * mg[:, CONV_WIDTH:AP_WIDTH]
    mixap_ref[0] = jnp.concatenate([an, pn], axis=-1).astype(BF16)


def _inverse_counts(seq_len, tile, n_tiles):
    lane = lax.broadcasted_iota(jnp.int32, (1, 1, POOL_WIDTH), 2)
    left = jnp.zeros_like(lane)
    right = jnp.zeros_like(lane)
    for g, w in enumerate(POOL_WINDOWS):
        in_group = (lane // POOL_GROUP_DIM) == g
        left = jnp.where(in_group, w // 2, left)
        right = jnp.where(in_group, w - 1 - w // 2, right)
    first_row = jnp.asarray([0, tile, (n_tiles - 1) * tile], jnp.int32)[:, None, None]
    pos = first_row + lax.broadcasted_iota(jnp.int32, (1, tile, 1), 1)
    cnt = jnp.minimum(pos + right, seq_len - 1) - jnp.maximum(pos - left, 0) + 1
    return 1.0 / jnp.maximum(cnt, 1).astype(F32)


def _in_call(h, layer, p, *, tile):
    B, L, D = h.shape
    nt = pl.cdiv(L, tile)
    assert nt >= 3 and (nt - 1) * tile + HALO <= L and tile >= HALO
    tb = tile // SUBLANES
    last8 = L // SUBLANES - 1
    const = lambda b, i: (layer, 0, 0)
    tile_kind = lambda b, i: (jnp.where(i == 0, 0, jnp.where(i == nt - 1, 2, 1)), 0, 0)
    return pl.pallas_call(
        functools.partial(_in_kernel, tile=tile, seq_len=L),
        grid=(B, nt),
        in_specs=[
            pl.BlockSpec((1, tile, D), lambda b, i: (b, i, 0)),
            pl.BlockSpec((1, HALO, D), lambda b, i: (b, jnp.maximum(i * tb - 1, 0), 0)),
            pl.BlockSpec((1, HALO, D), lambda b, i: (b, jnp.minimum((i + 1) * tb, last8), 0)),
            pl.BlockSpec((None, 1, D), const),
            pl.BlockSpec((None, D, IN_WIDTH), const),
            pl.BlockSpec((None, 3, CONV_WIDTH), const),
            pl.BlockSpec((None, POOL_WIDTH, POOL_WIDTH), const),
            pl.BlockSpec((None, 1, POOL_WIDTH), const),
            pl.BlockSpec((None, 1, D), const),
            pl.BlockSpec((1, tile, POOL_WIDTH), tile_kind),
        ],
        out_specs=[
            pl.BlockSpec((1, tile, AP_WIDTH), lambda b, i: (b, i, 0)),
            pl.BlockSpec((1, tile, FOURIER_WIDTH), lambda b, i: (b, i, 0)),
        ],
        out_shape=[
            jax.ShapeDtypeStruct((B, L, AP_WIDTH), BF16),
            jax.ShapeDtypeStruct((B, L, FOURIER_WIDTH), F32),
        ],
        compiler_params=pltpu.CompilerParams(
            dimension_semantics=("parallel", "parallel"), vmem_limit_bytes=VMEM_LIMIT_BYTES),
        name="in_proj_mixers",
    )(h, h, h, p["norm1_g"], p["w_in"], p["conv_w"], p["pool_bd"], p["pool_scale"], p["mix_g"],
      p["inv_cnt"])


def _dft_a_kernel(x_ref, ma_ref, o_ref, xs_ref, *, n2, n2p):
    for j in range(SUBLANES):
        xs_ref[j] = x_ref[0, :, j, :]
    for j in range(SUBLANES):
        hj = _dot(ma_ref[j], xs_ref[j].astype(BF16))
        o_ref[0, 0, :, j, :] = hj[:n2]
        o_ref[0, 1, :, j, :] = hj[n2p:n2p + n2]


def _dft_b_kernel(h_ref, wb_ref, c64_ref, s64_ref, o_ref, *, rows, chunk, scale):
    c64, s64 = c64_ref[...], s64_ref[...]
    for r0 in range(0, rows, chunk):
        m = min(chunk, rows - r0)
        wc, ws = wb_ref[0, :m, :m], wb_ref[1, :m, :m]
        hr = h_ref[0, 0, r0:r0 + m, :].astype(BF16)
        hi = h_ref[0, 1, r0:r0 + m, :].astype(BF16)
        yr = _dot(wc, hr) + _dot(ws, hi)
        yi = _dot(wc, hi) - _dot(ws, hr)
        f = _dot(yr.astype(BF16), c64) + _dot(yi.astype(BF16), s64)
        o_ref[0, r0:r0 + m, :] = f * scale


def _unpermute_kernel(x_ref, o_ref, *, n2):
    for j in range(SUBLANES):
        o_ref[0, j * n2:(j + 1) * n2, :] = x_ref[0, :, j, :]


def _dft_a_rows_kernel(x_ref, ma_ref, o_ref, xs_ref):
    for j in range(SUBLANES):
        xs_ref[j] = x_ref[0, :, j, :]
    for j in range(SUBLANES):
        o_ref[0, j] = _dot(ma_ref[j], xs_ref[j].astype(BF16)).astype(BF16)


def _dft_b_rows_kernel(h_ref, wb_ref, c64_ref, s64_ref, o_ref, *, n1, n2, n2p, scale):
    c64, s64 = c64_ref[...], s64_ref[...]
    wc, ws = wb_ref[0], wb_ref[1]
    run = BF16_ROWS
    for k0 in range(0, n2, run):
        keep = min(run, n2 - k0)
        hr = jnp.concatenate([h_ref[0, l1, k0:k0 + run, :] for l1 in range(n1)], axis=0)
        hi = jnp.concatenate([h_ref[0, l1, n2p + k0:n2p + k0 + run, :] for l1 in range(n1)], axis=0)
        yr = _dot(wc, hr) + _dot(ws, hi)
        yi = _dot(wc, hi) - _dot(ws, hr)
        f = (_dot(yr.astype(BF16), c64) + _dot(yi.astype(BF16), s64)) * scale
        for k1 in range(n1):
            o_ref[0, n2 * k1 + k0:n2 * k1 + k0 + keep, :] = f[run * k1:run * k1 + keep]


def _angle(idx, period):
    return (idx % period).astype(F32) * (2.0 * np.pi / period)


def _dft_tables(n1, n2, groups, l1_major):
    n = n1 * n2
    n2p = -(-n2 // BF16_ROWS) * BF16_ROWS
    l1, k2, l2 = (lax.broadcasted_iota(jnp.int32, (n1, n2p, n2), d) for d in range(3))
    ang = _angle(k2 * (l1 + n1 * l2), n)
    ma = jnp.concatenate([jnp.cos(ang), -jnp.sin(ang)], axis=1).astype(BF16)
    g = groups * n1
    r, c = (lax.broadcasted_iota(jnp.int32, (g, g), d) for d in range(2))
    if l1_major:
        angb = _angle((r // groups) * (c // groups), n1)
        same = (r % groups) == (c % groups)
    else:
        angb = _angle((r % n1) * (c % n1), n1)
        same = (r // n1) == (c // n1)
    wb = jnp.stack([jnp.where(same, jnp.cos(angb), 0.0), jnp.where(same, jnp.sin(angb), 0.0)]).astype(BF16)
    d, e = (lax.broadcasted_iota(jnp.int32, (FOURIER_WIDTH, FOURIER_WIDTH), d) for d in range(2))
    same_head = (d // FOURIER_HEAD_DIM) == (e // FOURIER_HEAD_DIM)
    ang64 = _angle(d * e, FOURIER_HEAD_DIM)
    return dict(ma=ma, wb=wb,
                c64=jnp.where(same_head, jnp.cos(ang64), 0.0).astype(BF16),
                s64=jnp.where(same_head, jnp.sin(ang64), 0.0).astype(BF16))


def _dft_rows_call(xf, tabs, *, n1, n2):
    B, L, W = xf.shape
    n2p = tabs["ma"].shape[1] // 2
    g = BF16_ROWS * n1
    h = pl.pallas_call(
        _dft_a_rows_kernel,
        grid=(B, n1 // SUBLANES),
        in_specs=[
            pl.BlockSpec((1, n2, SUBLANES, W), lambda b, i: (b, 0, i, 0)),
            pl.BlockSpec((SUBLANES, 2 * n2p, n2), lambda b, i: (i, 0, 0)),
        ],
        out_specs=pl.BlockSpec((1, SUBLANES, 2 * n2p, W), lambda b, i: (b, i, 0, 0)),
        out_shape=jax.ShapeDtypeStruct((B, n1, 2 * n2p, W), BF16),
        scratch_shapes=[pltpu.VMEM((SUBLANES, n2, W), F32)],
        compiler_params=pltpu.CompilerParams(
            dimension_semantics=("parallel", "parallel"), vmem_limit_bytes=VMEM_LIMIT_BYTES),
        name="seq_dft_stage_a_rows",
    )(xf.reshape(B, n2, n1, W), tabs["ma"])
    return pl.pallas_call(
        functools.partial(_dft_b_rows_kernel, n1=n1, n2=n2, n2p=n2p,
                          scale=float(1.0 / np.sqrt(L * FOURIER_HEAD_DIM))),
        grid=(B,),
        in_specs=[
            pl.BlockSpec((1, n1, 2 * n2p, W), lambda b: (b, 0, 0, 0)),
            pl.BlockSpec((2, g, g), lambda b: (0, 0, 0)),
            pl.BlockSpec((W, W), lambda b: (0, 0)),
            pl.BlockSpec((W, W), lambda b: (0, 0)),
        ],
        out_specs=pl.BlockSpec((1, L, W), lambda b: (b, 0, 0)),
        out_shape=jax.ShapeDtypeStruct((B, L, W), F32),
        compiler_params=pltpu.CompilerParams(
            dimension_semantics=("parallel",), vmem_limit_bytes=VMEM_LIMIT_BYTES),
        name="seq_dft_stage_b_rows",
    )(h, tabs["wb"], tabs["c64"], tabs["s64"])


def _dft_call(xf, tabs, *, n1, n2, groups, rows_b):
    B, L, W = xf.shape
    n2p = tabs["ma"].shape[1] // 2
    chunk = groups * n1
    params = pltpu.CompilerParams(
        dimension_semantics=("parallel", "parallel"), vmem_limit_bytes=VMEM_LIMIT_BYTES)
    nblk = n1 // SUBLANES
    h = pl.pallas_call(
        functools.partial(_dft_a_kernel, n2=n2, n2p=n2p),
        grid=(B, nblk),
        in_specs=[
            pl.BlockSpec((1, n2, SUBLANES, W), lambda b, i: (b, 0, i, 0)),
            pl.BlockSpec((SUBLANES, 2 * n2p, n2), lambda b, i: (i, 0, 0)),
        ],
        out_specs=pl.BlockSpec((1, 2, n2, SUBLANES, W), lambda b, i: (b, 0, 0, i, 0)),
        out_shape=jax.ShapeDtypeStruct((B, 2, n2, n1, W), F32),
        scratch_shapes=[pltpu.VMEM((SUBLANES, n2, W), F32)],
        compiler_params=params,
        name="seq_dft_stage_a",
    )(xf.reshape(B, n2, n1, W), tabs["ma"])
    const2 = lambda b, i: (0, 0)
    f = pl.pallas_call(
        functools.partial(_dft_b_kernel, rows=rows_b, chunk=chunk,
                          scale=float(1.0 / np.sqrt(L * FOURIER_HEAD_DIM))),
        grid=(B, pl.cdiv(L, rows_b)),
        in_specs=[
            pl.BlockSpec((1, 2, rows_b, W), lambda b, i: (b, 0, i, 0)),
            pl.BlockSpec((2, chunk, chunk), lambda b, i: (0, 0, 0)),
            pl.BlockSpec((W, W), const2),
            pl.BlockSpec((W, W), const2),
        ],
        out_specs=pl.BlockSpec((1, rows_b, W), lambda b, i: (b, i, 0)),
        out_shape=jax.ShapeDtypeStruct((B, L, W), F32),
        compiler_params=params,
        name="seq_dft_stage_b",
    )(h.reshape(B, 2, L, W), tabs["wb"], tabs["c64"], tabs["s64"])
    return pl.pallas_call(
        functools.partial(_unpermute_kernel, n2=n2),
        grid=(B, nblk),
        in_specs=[pl.BlockSpec((1, n2, SUBLANES, W), lambda b, i: (b, 0, i, 0))],
        out_specs=pl.BlockSpec((1, SUBLANES * n2, W), lambda b, i: (b, i, 0)),
        out_shape=jax.ShapeDtypeStruct((B, L, W), F32),
        compiler_params=params,
        name="seq_dft_unpermute",
    )(f.reshape(B, n2, n1, W))


def _out_kernel(h_ref, mixap_ref, f_ref, mgf_ref, wout_ref, g2_ref, wgu_ref, wdn_ref, fg_ref,
                o_ref, act_ref, *, last):
    f = f_ref[0]
    fn = (f * _rms_scale(f) * mgf_ref[...]).astype(BF16)
    mix = jnp.concatenate([mixap_ref[0], fn], axis=-1)
    h1 = h_ref[0] + _dot(mix, wout_ref[...])
    v = (h1 * _rms_scale(h1) * g2_ref[...]).astype(BF16)
    for c in range(D_FF // FF_CHUNK):
        lo = c * FF_CHUNK
        gate = _dot(v, wgu_ref[:, lo:lo + FF_CHUNK])
        up = _dot(v, wgu_ref[:, D_FF + lo:D_FF + lo + FF_CHUNK])
        act_ref[:, lo:lo + FF_CHUNK] = (gate * jax.nn.sigmoid(gate) * up).astype(BF16)
    h2 = h1 + _dot(act_ref[...], wdn_ref[...])
    if last:
        h2 = h2 * _rms_scale(h2) * fg_ref[...]
    o_ref[0] = h2


def _out_call(h, mixap, f, layer, p, *, tile, last):
    B, L, D = h.shape
    nt = pl.cdiv(L, tile)
    row = lambda b, i: (b, i, 0)
    const = lambda b, i: (layer, 0, 0)
    single = pl.Buffered(1)
    return pl.pallas_call(
        functools.partial(_out_kernel, last=last),
        grid=(B, nt),
        in_specs=[
            pl.BlockSpec((1, tile, D), row),
            pl.BlockSpec((1, tile, AP_WIDTH), row),
            pl.BlockSpec((1, tile, FOURIER_WIDTH), row),
            pl.BlockSpec((None, 1, FOURIER_WIDTH), const),
            pl.BlockSpec((None, D, D), const, pipeline_mode=single),
            pl.BlockSpec((None, 1, D), const),
            pl.BlockSpec((None, D, 2 * D_FF), const, pipeline_mode=single),
            pl.BlockSpec((None, D_FF, D), const, pipeline_mode=single),
            pl.BlockSpec((1, D), lambda b, i: (0, 0)),
        ],
        out_specs=pl.BlockSpec((1, tile, D), row),
        out_shape=jax.ShapeDtypeStruct((B, L, D), F32),
        scratch_shapes=[pltpu.VMEM((tile, D_FF), BF16)],
        compiler_params=pltpu.CompilerParams(
            dimension_semantics=("parallel", "parallel"), vmem_limit_bytes=VMEM_LIMIT_BYTES),
        name="out_proj_ffn",
    )(h, mixap, f, p["mix_g_f"], p["w_out"], p["norm2_g"], p["w_gate_up"], p["w_down"], p["final_g"])


def _trunk(x, meta_tokens, p, *, tile, n1, n2, groups=BF16_ROWS, rows_b=None):
    B, S, D = x.shape
    L = S + N_META
    assert n1 * n2 == L and n1 % SUBLANES == 0 and tile % BF16_ROWS == 0
    assert rows_b is None or rows_b % (groups * n1) in (0, L % (groups * n1))
    meta = jnp.broadcast_to(meta_tokens.astype(x.dtype)[None], (B, N_META, D))
    h = jnp.concatenate([meta, x], axis=1)
    tabs = _dft_tables(n1, n2, groups, l1_major=rows_b is None)
    p = dict(p, inv_cnt=_inverse_counts(L, tile, pl.cdiv(L, tile)))
    depth = p["w_in"].shape[0]
    for layer in range(depth):
        mixap, xf = _in_call(h, layer, p, tile=tile)
        if rows_b is None:
            f = _dft_rows_call(xf, tabs, n1=n1, n2=n2)
        else:
            f = _dft_call(xf, tabs, n1=n1, n2=n2, groups=groups, rows_b=rows_b)
        h = _out_call(h, mixap, f, layer, p, tile=tile, last=(layer == depth - 1))
    return h[:, N_META:]


def kernel(x_prompt, x_sample, meta_tokens, norm1_g, w_in, conv_w, pool_w, pool_scale, mix_g,
           w_out, norm2_g, w_gate_up, w_down, final_g):
    depth = w_in.shape[0]
    pool_bd = jnp.zeros((depth, POOL_WIDTH, POOL_WIDTH), F32)
    for g in range(POOL_GROUPS):
        sl = slice(g * POOL_GROUP_DIM, (g + 1) * POOL_GROUP_DIM)
        pool_bd = pool_bd.at[:, sl, sl].set(pool_w[:, g])
    p = dict(
        norm1_g=norm1_g[:, None, :],
        w_in=w_in.astype(BF16),
        conv_w=conv_w,
        pool_bd=pool_bd.astype(BF16),
        pool_scale=pool_scale[:, None, :],
        mix_g=mix_g[:, None, :],
        mix_g_f=mix_g[:, None, AP_WIDTH:],
        w_out=w_out.astype(BF16),
        norm2_g=norm2_g[:, None, :],
        w_gate_up=w_gate_up.astype(BF16),
        w_down=w_down.astype(BF16),
        final_g=final_g[None, :],
    )
    y_prompt = _trunk(x_prompt, meta_tokens, p, tile=656, n1=200, n2=82, groups=1, rows_b=2000)
    y_sample = _trunk(x_sample, meta_tokens, p, tile=688, n1=16, n2=257)
    return (y_prompt, y_sample)
```

```python
import functools

import numpy as np
import jax
import jax.numpy as jnp
from jax import lax
from jax.experimental import pallas as pl
from jax.experimental.pallas import tpu as pltpu

D_MODEL = 1024
N_META = 16
EPS = 1e-6
CONV_WIDTH = 384
POOL_WIDTH = 384
POOL_GROUPS = 4
POOL_GROUP_DIM = POOL_WIDTH // POOL_GROUPS
POOL_WINDOWS = (2, 4, 8, 16)
FOURIER_WIDTH = 256
FOURIER_HEAD_DIM = 64
AP_WIDTH = CONV_WIDTH + POOL_WIDTH
IN_WIDTH = 3 * CONV_WIDTH + POOL_WIDTH + FOURIER_WIDTH
D_FF = 2816
FF_CHUNK = 256

IN_COLUMN_ORDER = np.concatenate([np.arange(0, 384), np.arange(768, 1152), np.arange(1152, 1536),
                                  np.arange(384, 768), np.arange(1536, 1792)])
HALO = 8
SUBLANES = 8
LANES = 128
BF16_ROWS = 16
VMEM_LIMIT_BYTES = 56 * 1024 * 1024

F32 = jnp.float32
BF16 = jnp.bfloat16


def _rms_scale(x):
    return lax.rsqrt(jnp.mean(x * x, axis=-1, keepdims=True) + EPS)


def _dot(a, b):
    return jnp.dot(a, b, preferred_element_type=F32)


def _in_kernel(hm_ref, hp_ref, hn_ref, g1_ref, win_ref, cw_ref, pbd_ref, ps_ref, mg_ref, icnt_ref,
               mixap_ref, xf_ref, *, tile, seq_len):
    n = tile + 2 * HALO
    main = slice(HALO, HALO + tile)
    i = pl.program_id(1)
    hx = jnp.concatenate([hp_ref[0], hm_ref[0], hn_ref[0]], axis=0)
    u = (hx * _rms_scale(hx) * g1_ref[...]).astype(BF16)
    z = _dot(u, win_ref[...])
    c0 = CONV_WIDTH
    xa_cols, gc_cols, xp_cols, gb_cols = (slice(k * c0, (k + 1) * c0) for k in range(4))
    xf_ref[0] = z[main, 4 * c0:]
    pos = i * tile - HALO + lax.broadcasted_iota(jnp.int32, (n, 1), 0)
    valid = (pos >= 0) & (pos < seq_len)
    ga = jnp.where(valid, z[:, gc_cols] * z[:, xa_cols], 0.0)
    xp = jnp.where(valid, z[:, xp_cols], 0.0)

    def shift(x, s):
        return pltpu.roll(x, s % n, axis=0)

    cw = cw_ref[...]
    conv = shift(ga, 1)[main] * cw[0:1] + ga[main] * cw[1:2] + shift(ga, -1)[main] * cw[2:3]
    a = z[main, gb_cols] * conv

    rights = [w - 1 - w // 2 for w in POOL_WINDOWS]
    lane = lax.broadcasted_iota(jnp.int32, (1, LANES), 1)
    cols = []
    for c in range(POOL_WIDTH // LANES):
        x = xp[:, c * LANES:(c + 1) * LANES]
        t2 = x + shift(x, 1)
        t4 = t2 + shift(t2, 2)
        if c == 0:
            lo, hi = t2, shift(t4, -rights[1])
        else:
            t8 = t4 + shift(t4, 4)
            if c == 1:
                lo, hi = shift(t4, -rights[1]), shift(t8, -rights[2])
            else:
                t8d = shift(t8, 1)
                lo, hi = shift(t8, -rights[2]), t8d + shift(t8d, -8)
        boundary = (c + 1) * POOL_GROUP_DIM - c * LANES
        cols.append(jnp.where(lane < boundary, lo[main], hi[main]))
    wsum = jnp.concatenate(cols, axis=-1)
    pm = wsum * icnt_ref[0] - xp[main]
    p = _dot(pm.astype(BF16), pbd_ref[...]) * ps_ref[...]

    mg = mg_ref[...]
    an = a * _rms_scale(a) * mg[:, :CONV_WIDTH]
    pn = p * _rms_scale(p) * mg[:, CONV_WIDTH:AP_WIDTH]
    mixap_ref[0] = jnp.concatenate([an, pn], axis=-1).astype(BF16)


def _inverse_counts(seq_len, tile, n_tiles):
    lane = lax.broadcasted_iota(jnp.int32, (1, 1, POOL_WIDTH), 2)
    left = jnp.zeros_like(lane)
    right = jnp.zeros_like(lane)
    for g, w in enumerate(POOL_WINDOWS):
        in_group = (lane // POOL_GROUP_DIM) == g
        left = jnp.where(in_group, w // 2, left)
        right = jnp.where(in_group, w - 1 - w // 2, right)
    first_row = jnp.asarray([0, tile, (n_tiles - 1) * tile], jnp.int32)[:, None, None]
    pos = first_row + lax.broadcasted_iota(jnp.int32, (1, tile, 1), 1)
    cnt = jnp.minimum(pos + right, seq_len - 1) - jnp.maximum(pos - left, 0) + 1
    return 1.0 / jnp.maximum(cnt, 1).astype(F32)


def _in_call(h, layer, p, *, tile):
    B, L, D = h.shape
    nt = pl.cdiv(L, tile)
    assert nt >= 3 and (nt - 1) * tile + HALO <= L and tile >= HALO
    tb = tile // SUBLANES
    last8 = L // SUBLANES - 1
    const = lambda b, i: (layer, 0, 0)
    tile_kind = lambda b, i: (jnp.where(i == 0, 0, jnp.where(i == nt - 1, 2, 1)), 0, 0)
    return pl.pallas_call(
        functools.partial(_in_kernel, tile=tile, seq_len=L),
        grid=(B, nt),
        in_specs=[
            pl.BlockSpec((1, tile, D), lambda b, i: (b, i, 0)),
            pl.BlockSpec((1, HALO, D), lambda b, i: (b, jnp.maximum(i * tb - 1, 0), 0)),
            pl.BlockSpec((1, HALO, D), lambda b, i: (b, jnp.minimum((i + 1) * tb, last8), 0)),
            pl.BlockSpec((None, 1, D), const),
            pl.BlockSpec((None, D, IN_WIDTH), const),
            pl.BlockSpec((None, 3, CONV_WIDTH), const),
            pl.BlockSpec((None, POOL_WIDTH, POOL_WIDTH), const),
            pl.BlockSpec((None, 1, POOL_WIDTH), const),
            pl.BlockSpec((None, 1, D), const),
            pl.BlockSpec((1, tile, POOL_WIDTH), tile_kind),
        ],
        out_specs=[
            pl.BlockSpec((1, tile, AP_WIDTH), lambda b, i: (b, i, 0)),
            pl.BlockSpec((1, tile, FOURIER_WIDTH), lambda b, i: (b, i, 0)),
        ],
        out_shape=[
            jax.ShapeDtypeStruct((B, L, AP_WIDTH), BF16),
            jax.ShapeDtypeStruct((B, L, FOURIER_WIDTH), F32),
        ],
        compiler_params=pltpu.CompilerParams(
            dimension_semantics=("parallel", "parallel"), vmem_limit_bytes=VMEM_LIMIT_BYTES),
        name="in_proj_mixers",
    )(h, h, h, p["norm1_g"], p["w_in"], p["conv_w"], p["pool_bd"], p["pool_scale"], p["mix_g"],
      p["inv_cnt"])


def _dft_a_kernel(x_ref, ma_ref, o_ref, xs_ref, *, n2, n2p):
    for j in range(SUBLANES):
        xs_ref[j] = x_ref[0, :, j, :]
    for j in range(SUBLANES):
        hj = _dot(ma_ref[j], xs_ref[j].astype(BF16))
        o_ref[0, 0, :, j, :] = hj[:n2]
        o_ref[0, 1, :, j, :] = hj[n2p:n2p + n2]


def _dft_b_kernel(h_ref, wb_ref, c64_ref, s64_ref, o_ref, *, rows, chunk, scale):
    c64, s64 = c64_ref[...], s64_ref[...]
    for r0 in range(0, rows, chunk):
        m = min(chunk, rows - r0)
        wc, ws = wb_ref[0, :m, :m], wb_ref[1, :m, :m]
        hr = h_ref[0, 0, r0:r0 + m, :].astype(BF16)
        hi = h_ref[0, 1, r0:r0 + m, :].astype(BF16)
        yr = _dot(wc, hr) + _dot(ws, hi)
        yi = _dot(wc, hi) - _dot(ws, hr)
        f = _dot(yr.astype(BF16), c64) + _dot(yi.astype(BF16), s64)
        o_ref[0, r0:r0 + m, :] = f * scale


def _dft_a_rows_kernel(x_ref, ma_ref, o_ref, xs_ref):
    for j in range(SUBLANES):
        xs_ref[j] = x_ref[0, :, j, :]
    for j in range(SUBLANES):
        o_ref[0, j] = _dot(ma_ref[j], xs_ref[j].astype(BF16)).astype(BF16)


def _dft_b_rows_kernel(h_ref, wb_ref, c64_ref, s64_ref, o_ref, *, n1, n2, n2p, scale):
    c64, s64 = c64_ref[...], s64_ref[...]
    wc, ws = wb_ref[0], wb_ref[1]
    run = BF16_ROWS
    for k0 in range(0, n2, run):
        keep = min(run, n2 - k0)
        hr = jnp.concatenate([h_ref[0, l1, k0:k0 + run, :] for l1 in range(n1)], axis=0)
        hi = jnp.concatenate([h_ref[0, l1, n2p + k0:n2p + k0 + run, :] for l1 in range(n1)], axis=0)
        yr = _dot(wc, hr) + _dot(ws, hi)
        yi = _dot(wc, hi) - _dot(ws, hr)
        f = (_dot(yr.astype(BF16), c64) + _dot(yi.astype(BF16), s64)) * scale
        for k1 in range(n1):
            o_ref[0, n2 * k1 + k0:n2 * k1 + k0 + keep, :] = f[run * k1:run * k1 + keep]


def _angle(idx, period):
    return (idx % period).astype(F32) * (2.0 * np.pi / period)


def _dft_tables(n1, n2, groups, l1_major):
    n = n1 * n2
    n2p = -(-n2 // BF16_ROWS) * BF16_ROWS
    l1, k2 = (lax.broadcasted_iota(jnp.int32, (n1, n2p, 1), d) for d in range(2))
    k2b, l2 = (lax.broadcasted_iota(jnp.int32, (1, n2p, n2), d) for d in (1, 2))
    a, b = _angle(k2 * l1, n), _angle(k2b * l2, n2)
    ca, sa, cb, sb = jnp.cos(a), jnp.sin(a), jnp.cos(b), jnp.sin(b)
    ma = jnp.concatenate([ca * cb - sa * sb, -(sa * cb + ca * sb)], axis=1).astype(BF16)
    g = groups * n1
    r, c = (lax.broadcasted_iota(jnp.int32, (g, g), d) for d in range(2))
    if l1_major:
        angb = _angle((r // groups) * (c // groups), n1)
        same = (r % groups) == (c % groups)
    else:
        angb = _angle((r % n1) * (c % n1), n1)
        same = (r // n1) == (c // n1)
    wb = jnp.stack([jnp.where(same, jnp.cos(angb), 0.0), jnp.where(same, jnp.sin(angb), 0.0)]).astype(BF16)
    d, e = (lax.broadcasted_iota(jnp.int32, (FOURIER_WIDTH, FOURIER_WIDTH), d) for d in range(2))
    same_head = (d // FOURIER_HEAD_DIM) == (e // FOURIER_HEAD_DIM)
    ang64 = _angle(d * e, FOURIER_HEAD_DIM)
    return dict(ma=ma, wb=wb,
                c64=jnp.where(same_head, jnp.cos(ang64), 0.0).astype(BF16),
                s64=jnp.where(same_head, jnp.sin(ang64), 0.0).astype(BF16))


def _dft_rows_call(xf, tabs, *, n1, n2):
    B, L, W = xf.shape
    n2p = tabs["ma"].shape[1] // 2
    g = BF16_ROWS * n1
    h = pl.pallas_call(
        _dft_a_rows_kernel,
        grid=(n1 // SUBLANES, B),
        in_specs=[
            pl.BlockSpec((1, n2, SUBLANES, W), lambda i, b: (b, 0, i, 0)),
            pl.BlockSpec((SUBLANES, 2 * n2p, n2), lambda i, b: (i, 0, 0)),
        ],
        out_specs=pl.BlockSpec((1, SUBLANES, 2 * n2p, W), lambda i, b: (b, i, 0, 0)),
        out_shape=jax.ShapeDtypeStruct((B, n1, 2 * n2p, W), BF16),
        scratch_shapes=[pltpu.VMEM((SUBLANES, n2, W), F32)],
        compiler_params=pltpu.CompilerParams(
            dimension_semantics=("parallel", "parallel"), vmem_limit_bytes=VMEM_LIMIT_BYTES),
        name="seq_dft_stage_a_rows",
    )(xf.reshape(B, n2, n1, W), tabs["ma"])
    return pl.pallas_call(
        functools.partial(_dft_b_rows_kernel, n1=n1, n2=n2, n2p=n2p,
                          scale=float(1.0 / np.sqrt(L * FOURIER_HEAD_DIM))),
        grid=(B,),
        in_specs=[
            pl.BlockSpec((1, n1, 2 * n2p, W), lambda b: (b, 0, 0, 0)),
            pl.BlockSpec((2, g, g), lambda b: (0, 0, 0)),
            pl.BlockSpec((W, W), lambda b: (0, 0)),
            pl.BlockSpec((W, W), lambda b: (0, 0)),
        ],
        out_specs=pl.BlockSpec((1, L, W), lambda b: (b, 0, 0)),
        out_shape=jax.ShapeDtypeStruct((B, L, W), F32),
        compiler_params=pltpu.CompilerParams(
            dimension_semantics=("parallel",), vmem_limit_bytes=VMEM_LIMIT_BYTES),
        name="seq_dft_stage_b_rows",
    )(h, tabs["wb"], tabs["c64"], tabs["s64"])


def _dft_call(xf, tabs, *, n1, n2, groups, rows_b):
    B, L, W = xf.shape
    n2p = tabs["ma"].shape[1] // 2
    chunk = groups * n1
    params = pltpu.CompilerParams(
        dimension_semantics=("parallel", "parallel"), vmem_limit_bytes=VMEM_LIMIT_BYTES)
    nblk = n1 // SUBLANES
    h = pl.pallas_call(
        functools.partial(_dft_a_kernel, n2=n2, n2p=n2p),
        grid=(B, nblk),
        in_specs=[
            pl.BlockSpec((1, n2, SUBLANES, W), lambda b, i: (b, 0, i, 0)),
            pl.BlockSpec((SUBLANES, 2 * n2p, n2), lambda b, i: (i, 0, 0)),
        ],
        out_specs=pl.BlockSpec((1, 2, n2, SUBLANES, W), lambda b, i: (b, 0, 0, i, 0)),
        out_shape=jax.ShapeDtypeStruct((B, 2, n2, n1, W), F32),
        scratch_shapes=[pltpu.VMEM((SUBLANES, n2, W), F32)],
        compiler_params=params,
        name="seq_dft_stage_a",
    )(xf.reshape(B, n2, n1, W), tabs["ma"])
    const2 = lambda b, i: (0, 0)
    f = pl.pallas_call(
        functools.partial(_dft_b_kernel, rows=rows_b, chunk=chunk,
                          scale=float(1.0 / np.sqrt(L * FOURIER_HEAD_DIM))),
        grid=(B, pl.cdiv(L, rows_b)),
        in_specs=[
            pl.BlockSpec((1, 2, rows_b, W), lambda b, i: (b, 0, i, 0)),
            pl.BlockSpec((2, chunk, chunk), lambda b, i: (0, 0, 0)),
            pl.BlockSpec((W, W), const2),
            pl.BlockSpec((W, W), const2),
        ],
        out_specs=pl.BlockSpec((1, rows_b, W), lambda b, i: (b, i, 0)),
        out_shape=jax.ShapeDtypeStruct((B, L, W), F32),
        compiler_params=params,
        name="seq_dft_stage_b",
    )(h.reshape(B, 2, L, W), tabs["wb"], tabs["c64"], tabs["s64"])
    return f.reshape(B, n2, n1, W)


def _out_kernel(h_ref, mixap_ref, f_ref, mgf_ref, wout_ref, g2_ref, wgu_ref, wdn_ref, fg_ref,
                o_ref, act_ref, *scratch, last, permuted_n2):
    if permuted_n2 is None:
        f = f_ref[0]
    else:
        fbuf_ref, = scratch
        for j in range(SUBLANES):
            fbuf_ref[j * permuted_n2:(j + 1) * permuted_n2, :] = f_ref[0, :, j, :]
        f = fbuf_ref[...]
    fn = (f * _rms_scale(f) * mgf_ref[...]).astype(BF16)
    mix = jnp.concatenate([mixap_ref[0], fn], axis=-1)
    h1 = h_ref[0] + _dot(mix, wout_ref[...])
    v = (h1 * _rms_scale(h1) * g2_ref[...]).astype(BF16)
    for c in range(D_FF // FF_CHUNK):
        lo = c * FF_CHUNK
        gate = _dot(v, wgu_ref[:, lo:lo + FF_CHUNK])
        up = _dot(v, wgu_ref[:, D_FF + lo:D_FF + lo + FF_CHUNK])
        act_ref[:, lo:lo + FF_CHUNK] = (gate * jax.nn.sigmoid(gate) * up).astype(BF16)
    h2 = h1 + _dot(act_ref[...], wdn_ref[...])
    if last:
        h2 = h2 * _rms_scale(h2) * fg_ref[...]
    o_ref[0] = h2


def _out_call(h, mixap, f, layer, p, *, tile, last):
    B, L, D = h.shape
    nt = pl.cdiv(L, tile)
    row = lambda b, i: (b, i, 0)
    const = lambda b, i: (layer, 0, 0)
    single = pl.Buffered(1)
    scratch = [pltpu.VMEM((tile, D_FF), BF16)]
    if f.ndim == 4:
        permuted_n2 = f.shape[1]
        assert tile == SUBLANES * permuted_n2 and f.shape[2] * permuted_n2 == L
        f_spec = pl.BlockSpec((1, permuted_n2, SUBLANES, FOURIER_WIDTH), lambda b, i: (b, 0, i, 0))
        scratch.append(pltpu.VMEM((tile, FOURIER_WIDTH), F32))
    else:
        permuted_n2 = None
        f_spec = pl.BlockSpec((1, tile, FOURIER_WIDTH), row)
    return pl.pallas_call(
        functools.partial(_out_kernel, last=last, permuted_n2=permuted_n2),
        grid=(B, nt),
        in_specs=[
            pl.BlockSpec((1, tile, D), row),
            pl.BlockSpec((1, tile, AP_WIDTH), row),
            f_spec,
            pl.BlockSpec((None, 1, FOURIER_WIDTH), const),
            pl.BlockSpec((None, D, D), const, pipeline_mode=single),
            pl.BlockSpec((None, 1, D), const),
            pl.BlockSpec((None, D, 2 * D_FF), const, pipeline_mode=single),
            pl.BlockSpec((None, D_FF, D), const, pipeline_mode=single),
            pl.BlockSpec((1, D), lambda b, i: (0, 0)),
        ],
        out_specs=pl.BlockSpec((1, tile, D), row),
        out_shape=jax.ShapeDtypeStruct((B, L, D), F32),
        scratch_shapes=scratch,
        compiler_params=pltpu.CompilerParams(
            dimension_semantics=("parallel", "parallel"), vmem_limit_bytes=VMEM_LIMIT_BYTES),
        name="out_proj_ffn",
    )(h, mixap, f, p["mix_g_f"], p["w_out"], p["norm2_g"], p["w_gate_up"], p["w_down"], p["final_g"])


def _trunk(x, meta_tokens, p, *, tile, n1, n2, groups=BF16_ROWS, rows_b=None):
    B, S, D = x.shape
    L = S + N_META
    assert n1 * n2 == L and n1 % SUBLANES == 0 and tile % BF16_ROWS == 0
    assert rows_b is None or rows_b % (groups * n1) in (0, L % (groups * n1))
    meta = jnp.broadcast_to(meta_tokens.astype(x.dtype)[None], (B, N_META, D))
    h = jnp.concatenate([meta, x], axis=1)
    tabs = _dft_tables(n1, n2, groups, l1_major=rows_b is None)
    p = dict(p, inv_cnt=_inverse_counts(L, tile, pl.cdiv(L, tile)))
    depth = p["w_in"].shape[0]
    for layer in range(depth):
        mixap, xf = _in_call(h, layer, p, tile=tile)
        if rows_b is None:
            f = _dft_rows_call(xf, tabs, n1=n1, n2=n2)
        else:
            f = _dft_call(xf, tabs, n1=n1, n2=n2, groups=groups, rows_b=rows_b)
        h = _out_call(h, mixap, f, layer, p, tile=tile, last=(layer == depth - 1))
    return h[:, N_META:]


def kernel(x_prompt, x_sample, meta_tokens, norm1_g, w_in, conv_w, pool_w, pool_scale, mix_g,
           w_out, norm2_g, w_gate_up, w_down, final_g):
    depth = w_in.shape[0]
    pool_bd = jnp.zeros((depth, POOL_WIDTH, POOL_WIDTH), F32)
    for g in range(POOL_GROUPS):
        sl = slice(g * POOL_GROUP_DIM, (g + 1) * POOL_GROUP_DIM)
        pool_bd = pool_bd.at[:, sl, sl].set(pool_w[:, g])
    p = dict(
        norm1_g=norm1_g[:, None, :],
        w_in=w_in[:, :, IN_COLUMN_ORDER].astype(BF16),
        conv_w=conv_w,
        pool_bd=pool_bd.astype(BF16),
        pool_scale=pool_scale[:, None, :],
        mix_g=mix_g[:, None, :],
        mix_g_f=mix_g[:, None, AP_WIDTH:],
        w_out=w_out.astype(BF16),
        norm2_g=norm2_g[:, None, :],
        w_gate_up=w_gate_up.astype(BF16),
        w_down=w_down.astype(BF16),
        final_g=final_g[None, :],
    )
    y_prompt = _trunk(x_prompt, meta_tokens, p, tile=656, n1=200, n2=82, groups=1, rows_b=2000)
    y_sample = _trunk(x_sample, meta_tokens, p, tile=688, n1=16, n2=257)
    return (y_prompt, y_sample)
```

```python
import functools

import numpy as np
import jax
import jax.numpy as jnp
from jax import lax
from jax.experimental import pallas as pl
from jax.experimental.pallas import tpu as pltpu

D_MODEL = 1024
N_META = 16
EPS = 1e-6
CONV_WIDTH = 384
POOL_WIDTH = 384
POOL_GROUPS = 4
POOL_GROUP_DIM = POOL_WIDTH // POOL_GROUPS
POOL_WINDOWS = (2, 4, 8, 16)
FOURIER_WIDTH = 256
FOURIER_HEAD_DIM = 64
AP_WIDTH = CONV_WIDTH + POOL_WIDTH
IN_WIDTH = 3 * CONV_WIDTH + POOL_WIDTH + FOURIER_WIDTH
D_FF = 2816
FF_CHUNK = 256

HALO = 8
SUBLANES = 8
LANES = 128
BF16_ROWS = 16
VMEM_LIMIT_BYTES = 56 * 1024 * 1024

F32 = jnp.float32
BF16 = jnp.bfloat16


def _rms_scale(x):
    return lax.rsqrt(jnp.mean(x * x, axis=-1, keepdims=True) + EPS)


def _dot(a, b):
    return jnp.dot(a, b, preferred_element_type=F32)


def _first_layer_specs(x, tile):
    B, S, D = x.shape
    e = N_META
    assert e == BF16_ROWS and tile % e == 0 and pl.cdiv(S + e, tile) * tile - 2 * e <= S
    te = tile // e
    return [
        pl.BlockSpec((1, e, D), lambda b, i: (b, jnp.maximum(i * te - 1, 0), 0)),
        pl.BlockSpec((pl.Element(1), pl.Element(tile - 2 * e), pl.Element(D)),
                     lambda b, i: (b, SUBLANES * (i * (tile // SUBLANES)), 0)),
        pl.BlockSpec((1, e, D), lambda b, i: (b, jnp.minimum((i + 1) * te - 2, S // e - 1), 0)),
        pl.BlockSpec((e, D), lambda b, i: (0, 0)),
    ]


def _first_layer_tile(head_ref, mid_ref, tail_ref, meta_ref):
    head = jnp.where(pl.program_id(1) == 0, meta_ref[...], head_ref[0])
    return jnp.concatenate([head, mid_ref[0], tail_ref[0]], axis=0)


def _in_kernel(*refs, tile, seq_len, first):
    n_tile_refs = 4 if first else 1
    hp_ref, hn_ref, g1_ref, win_ref, cw_ref, pbd_ref, ps_ref, mg_ref, icnt_ref, mixap_ref, xf_ref = (
        refs[n_tile_refs:])
    hm = _first_layer_tile(*refs[:4]) if first else refs[0][0]
    n = tile + 2 * HALO
    main = slice(HALO, HALO + tile)
    i = pl.program_id(1)
    hx = jnp.concatenate([hp_ref[0], hm, hn_ref[0]], axis=0)
    u = (hx * _rms_scale(hx) * g1_ref[...]).astype(BF16)
    z = _dot(u, win_ref[...])
    c0 = CONV_WIDTH
    xa_cols, gc_cols, xp_cols, gb_cols = (slice(k * c0, (k + 1) * c0) for k in range(4))
    xf_ref[0] = z[main, 4 * c0:]
    pos = i * tile - HALO + lax.broadcasted_iota(jnp.int32, (n, 1), 0)
    valid = (pos >= 0) & (pos < seq_len)
    ga = jnp.where(valid, z[:, gc_cols] * z[:, xa_cols], 0.0)
    xp = jnp.where(valid, z[:, xp_cols], 0.0)

    def shift(x, s):
        return pltpu.roll(x, s % n, axis=0)

    cw = cw_ref[...]
    conv = shift(ga, 1)[main] * cw[0:1] + ga[main] * cw[1:2] + shift(ga, -1)[main] * cw[2:3]
    a = z[main, gb_cols] * conv

    rights = [w - 1 - w // 2 for w in POOL_WINDOWS]
    lane = lax.broadcasted_iota(jnp.int32, (1, LANES), 1)
    cols = []
    for c in range(POOL_WIDTH // LANES):
        x = xp[:, c * LANES:(c + 1) * LANES]
        t2 = x + shift(x, 1)
        t4 = t2 + shift(t2, 2)
        if c == 0:
            lo, hi = t2, shift(t4, -rights[1])
        else:
            t8 = t4 + shift(t4, 4)
            if c == 1:
                lo, hi = shift(t4, -rights[1]), shift(t8, -rights[2])
            else:
                t8d = shift(t8, 1)
                lo, hi = shift(t8, -rights[2]), t8d + shift(t8d, -8)
        boundary = (c + 1) * POOL_GROUP_DIM - c * LANES
        cols.append(jnp.where(lane < boundary, lo[main], hi[main]))
    wsum = jnp.concatenate(cols, axis=-1)
    pm = wsum * icnt_ref[0] - xp[main]
    p = _dot(pm.astype(BF16), pbd_ref[...]) * ps_ref[...]

    mg = mg_ref[...]
    an = a * _rms_scale(a) * mg[:, :CONV_WIDTH]
    pn = p * _rms_scale(p) * mg[:, CONV_WIDTH:AP_WIDTH]
    mixap_ref[0] = jnp.concatenate([an, pn], axis=-1).astype(BF16)


def _inverse_counts(seq_len, tile, n_tiles):
    lane = lax.broadcasted_iota(jnp.int32, (1, 1, POOL_WIDTH), 2)
    left = jnp.zeros_like(lane)
    right = jnp.zeros_like(lane)
    for g, w in enumerate(POOL_WINDOWS):
        in_group = (lane // POOL_GROUP_DIM) == g
        left = jnp.where(in_group, w // 2, left)
        right = jnp.where(in_group, w - 1 - w // 2, right)
    first_row = jnp.asarray([0, tile, (n_tiles - 1) * tile], jnp.int32)[:, None, None]
    pos = first_row + lax.broadcasted_iota(jnp.int32, (1, tile, 1), 1)
    cnt = jnp.minimum(pos + right, seq_len - 1) - jnp.maximum(pos - left, 0) + 1
    return 1.0 / jnp.maximum(cnt, 1).astype(F32)


def _in_call(h, layer, p, *, tile, meta=None):
    first = meta is not None
    B, rows, D = h.shape
    L = rows + N_META if first else rows
    nt = pl.cdiv(L, tile)
    assert nt >= 3 and (nt - 1) * tile + HALO <= L and tile >= HALO
    tb = tile // SUBLANES
    last8 = rows // SUBLANES - 1
    shift8 = (L - rows) // SUBLANES
    const = lambda b, i: (layer, 0, 0)
    tile_kind = lambda b, i: (jnp.where(i == 0, 0, jnp.where(i == nt - 1, 2, 1)), 0, 0)
    if first:
        tile_specs, tile_args = _first_layer_specs(h, tile), (h, h, h, meta)
    else:
        tile_specs, tile_args = [pl.BlockSpec((1, tile, D), lambda b, i: (b, i, 0))], (h,)
    return pl.pallas_call(
        functools.partial(_in_kernel, tile=tile, seq_len=L, first=first),
        grid=(B, nt),
        in_specs=tile_specs + [
            pl.BlockSpec((1, HALO, D), lambda b, i: (b, jnp.maximum(i * tb - 1 - shift8, 0), 0)),
            pl.BlockSpec((1, HALO, D), lambda b, i: (b, jnp.minimum((i + 1) * tb - shift8, last8), 0)),
            pl.BlockSpec((None, 1, D), const),
            pl.BlockSpec((None, D, IN_WIDTH), const),
            pl.BlockSpec((None, 3, CONV_WIDTH), const),
            pl.BlockSpec((None, POOL_WIDTH, POOL_WIDTH), const),
            pl.BlockSpec((None, 1, POOL_WIDTH), const),
            pl.BlockSpec((None, 1, D), const),
            pl.BlockSpec((1, tile, POOL_WIDTH), tile_kind),
        ],
        out_specs=[
            pl.BlockSpec((1, tile, AP_WIDTH), lambda b, i: (b, i, 0)),
            pl.BlockSpec((1, tile, FOURIER_WIDTH), lambda b, i: (b, i, 0)),
        ],
        out_shape=[
            jax.ShapeDtypeStruct((B, L, AP_WIDTH), BF16),
            jax.ShapeDtypeStruct((B, L, FOURIER_WIDTH), F32),
        ],
        compiler_params=pltpu.CompilerParams(
            dimension_semantics=("parallel", "parallel"), vmem_limit_bytes=VMEM_LIMIT_BYTES),
        name="in_proj_mixers",
    )(*tile_args, h, h, p["norm1_g"], p["w_in"], p["conv_w"], p["pool_bd"], p["pool_scale"], p["mix_g"],
      p["inv_cnt"])


def _dft_a_kernel(x_ref, ma_ref, o_ref, xs_ref, *, n2, n2p):
    for j in range(SUBLANES):
        xs_ref[j] = x_ref[0, :, j, :]
    for j in range(SUBLANES):
        hj = _dot(ma_ref[j], xs_ref[j].astype(BF16))
        o_ref[0, 0, :, j, :] = hj[:n2]
        o_ref[0, 1, :, j, :] = hj[n2p:n2p + n2]


def _dft_b_kernel(h_ref, wb_ref, c64_ref, s64_ref, o_ref, *, rows, chunk, scale):
    c64, s64 = c64_ref[...], s64_ref[...]
    for r0 in range(0, rows, chunk):
        m = min(chunk, rows - r0)
        wc, ws = wb_ref[0, :m, :m], wb_ref[1, :m, :m]
        hr = h_ref[0, 0, r0:r0 + m, :].astype(BF16)
        hi = h_ref[0, 1, r0:r0 + m, :].astype(BF16)
        yr = _dot(wc, hr) + _dot(ws, hi)
        yi = _dot(wc, hi) - _dot(ws, hr)
        f = _dot(yr.astype(BF16), c64) + _dot(yi.astype(BF16), s64)
        o_ref[0, r0:r0 + m, :] = f * scale


def _unpermute_kernel(x_ref, o_ref, *, n2):
    for j in range(SUBLANES):
        o_ref[0, j * n2:(j + 1) * n2, :] = x_ref[0, :, j, :]


def _dft_a_rows_kernel(x_ref, ma_ref, o_ref, xs_ref):
    for j in range(SUBLANES):
        xs_ref[j] = x_ref[0, :, j, :]
    for j in range(SUBLANES):
        o_ref[0, j] = _dot(ma_ref[j], xs_ref[j].astype(BF16)).astype(BF16)


def _dft_b_rows_kernel(h_ref, wb_ref, c64_ref, s64_ref, o_ref, *, n1, n2, n2p, scale):
    c64, s64 = c64_ref[...], s64_ref[...]
    wc, ws = wb_ref[0], wb_ref[1]
    run = BF16_ROWS
    for k0 in range(0, n2, run):
        keep = min(run, n2 - k0)
        hr = jnp.concatenate([h_ref[0, l1, k0:k0 + run, :] for l1 in range(n1)], axis=0)
        hi = jnp.concatenate([h_ref[0, l1, n2p + k0:n2p + k0 + run, :] for l1 in range(n1)], axis=0)
        yr = _dot(wc, hr) + _dot(ws, hi)
        yi = _dot(wc, hi) - _dot(ws, hr)
        f = (_dot(yr.astype(BF16), c64) + _dot(yi.astype(BF16), s64)) * scale
        for k1 in range(n1):
            o_ref[0, n2 * k1 + k0:n2 * k1 + k0 + keep, :] = f[run * k1:run * k1 + keep]


def _angle(idx, period):
    return (idx % period).astype(F32) * (2.0 * np.pi / period)


def _dft_tables(n1, n2, groups, l1_major):
    n = n1 * n2
    n2p = -(-n2 // BF16_ROWS) * BF16_ROWS
    l1, k2 = (lax.broadcasted_iota(jnp.int32, (n1, n2p, 1), d) for d in range(2))
    k2b, l2 = (lax.broadcasted_iota(jnp.int32, (1, n2p, n2), d) for d in (1, 2))
    a, b = _angle(k2 * l1, n), _angle(k2b * l2, n2)
    ca, sa, cb, sb = jnp.cos(a), jnp.sin(a), jnp.cos(b), jnp.sin(b)
    ma = jnp.concatenate([ca * cb - sa * sb, -(sa * cb + ca * sb)], axis=1).astype(BF16)
    g = groups * n1
    r, c = (lax.broadcasted_iota(jnp.int32, (g, g), d) for d in range(2))
    if l1_major:
        angb = _angle((r // groups) * (c // groups), n1)
        same = (r % groups) == (c % groups)
    else:
        angb = _angle((r % n1) * (c % n1), n1)
        same = (r // n1) == (c // n1)
    wb = jnp.stack([jnp.where(same, jnp.cos(angb), 0.0), jnp.where(same, jnp.sin(angb), 0.0)]).astype(BF16)
    d, e = (lax.broadcasted_iota(jnp.int32, (FOURIER_WIDTH, FOURIER_WIDTH), d) for d in range(2))
    same_head = (d // FOURIER_HEAD_DIM) == (e // FOURIER_HEAD_DIM)
    ang64 = _angle(d * e, FOURIER_HEAD_DIM)
    return dict(ma=ma, wb=wb,
                c64=jnp.where(same_head, jnp.cos(ang64), 0.0).astype(BF16),
                s64=jnp.where(same_head, jnp.sin(ang64), 0.0).astype(BF16))


def _dft_rows_call(xf, tabs, *, n1, n2):
    B, L, W = xf.shape
    n2p = tabs["ma"].shape[1] // 2
    g = BF16_ROWS * n1
    h = pl.pallas_call(
        _dft_a_rows_kernel,
        grid=(n1 // SUBLANES, B),
        in_specs=[
            pl.BlockSpec((1, n2, SUBLANES, W), lambda i, b: (b, 0, i, 0)),
            pl.BlockSpec((SUBLANES, 2 * n2p, n2), lambda i, b: (i, 0, 0)),
        ],
        out_specs=pl.BlockSpec((1, SUBLANES, 2 * n2p, W), lambda i, b: (b, i, 0, 0)),
        out_shape=jax.ShapeDtypeStruct((B, n1, 2 * n2p, W), BF16),
        scratch_shapes=[pltpu.VMEM((SUBLANES, n2, W), F32)],
        compiler_params=pltpu.CompilerParams(
            dimension_semantics=("parallel", "parallel"), vmem_limit_bytes=VMEM_LIMIT_BYTES),
        name="seq_dft_stage_a_rows",
    )(xf.reshape(B, n2, n1, W), tabs["ma"])
    return pl.pallas_call(
        functools.partial(_dft_b_rows_kernel, n1=n1, n2=n2, n2p=n2p,
                          scale=float(1.0 / np.sqrt(L * FOURIER_HEAD_DIM))),
        grid=(B,),
        in_specs=[
            pl.BlockSpec((1, n1, 2 * n2p, W), lambda b: (b, 0, 0, 0)),
            pl.BlockSpec((2, g, g), lambda b: (0, 0, 0)),
            pl.BlockSpec((W, W), lambda b: (0, 0)),
            pl.BlockSpec((W, W), lambda b: (0, 0)),
        ],
        out_specs=pl.BlockSpec((1, L, W), lambda b: (b, 0, 0)),
        out_shape=jax.ShapeDtypeStruct((B, L, W), F32),
        compiler_params=pltpu.CompilerParams(
            dimension_semantics=("parallel",), vmem_limit_bytes=VMEM_LIMIT_BYTES),
        name="seq_dft_stage_b_rows",
    )(h, tabs["wb"], tabs["c64"], tabs["s64"])


def _dft_call(xf, tabs, *, n1, n2, groups, rows_b, natural):
    B, L, W = xf.shape
    n2p = tabs["ma"].shape[1] // 2
    chunk = groups * n1
    params = pltpu.CompilerParams(
        dimension_semantics=("parallel", "parallel"), vmem_limit_bytes=VMEM_LIMIT_BYTES)
    nblk = n1 // SUBLANES
    h = pl.pallas_call(
        functools.partial(_dft_a_kernel, n2=n2, n2p=n2p),
        grid=(B, nblk),
        in_specs=[
            pl.BlockSpec((1, n2, SUBLANES, W), lambda b, i: (b, 0, i, 0)),
            pl.BlockSpec((SUBLANES, 2 * n2p, n2), lambda b, i: (i, 0, 0)),
        ],
        out_specs=pl.BlockSpec((1, 2, n2, SUBLANES, W), lambda b, i: (b, 0, 0, i, 0)),
        out_shape=jax.ShapeDtypeStruct((B, 2, n2, n1, W), F32),
        scratch_shapes=[pltpu.VMEM((SUBLANES, n2, W), F32)],
        compiler_params=params,
        name="seq_dft_stage_a",
    )(xf.reshape(B, n2, n1, W), tabs["ma"])
    const2 = lambda b, i: (0, 0)
    f = pl.pallas_call(
        functools.partial(_dft_b_kernel, rows=rows_b, chunk=chunk,
                          scale=float(1.0 / np.sqrt(L * FOURIER_HEAD_DIM))),
        grid=(B, pl.cdiv(L, rows_b)),
        in_specs=[
            pl.BlockSpec((1, 2, rows_b, W), lambda b, i: (b, 0, i, 0)),
            pl.BlockSpec((2, chunk, chunk), lambda b, i: (0, 0, 0)),
            pl.BlockSpec((W, W), const2),
            pl.BlockSpec((W, W), const2),
        ],
        out_specs=pl.BlockSpec((1, rows_b, W), lambda b, i: (b, i, 0)),
        out_shape=jax.ShapeDtypeStruct((B, L, W), F32),
        compiler_params=params,
        name="seq_dft_stage_b",
    )(h.reshape(B, 2, L, W), tabs["wb"], tabs["c64"], tabs["s64"])
    f = f.reshape(B, n2, n1, W)
    if not natural:
        return f
    return pl.pallas_call(
        functools.partial(_unpermute_kernel, n2=n2),
        grid=(B, nblk),
        in_specs=[pl.BlockSpec((1, n2, SUBLANES, W), lambda b, i: (b, 0, i, 0))],
        out_specs=pl.BlockSpec((1, SUBLANES * n2, W), lambda b, i: (b, i, 0)),
        out_shape=jax.ShapeDtypeStruct((B, L, W), F32),
        compiler_params=params,
        name="seq_dft_unpermute",
    )(f)


def _out_kernel(*refs, last, first, permuted_n2):
    n_tile_refs = 4 if first else 1
    mixap_ref, f_ref, mgf_ref, wout_ref, g2_ref, wgu_ref, wdn_ref, fg_ref, o_ref, act_ref, *scratch = (
        refs[n_tile_refs:])
    h = _first_layer_tile(*refs[:4]) if first else refs[0][0]
    if permuted_n2 is None:
        f = f_ref[0]
    else:
        fbuf_ref, = scratch
        for j in range(SUBLANES):
            fbuf_ref[j * permuted_n2:(j + 1) * permuted_n2, :] = f_ref[0, :, j, :]
        f = fbuf_ref[...]
    fn = (f * _rms_scale(f) * mgf_ref[...]).astype(BF16)
    mix = jnp.concatenate([mixap_ref[0], fn], axis=-1)
    h1 = h + _dot(mix, wout_ref[...])
    v = (h1 * _rms_scale(h1) * g2_ref[...]).astype(BF16)
    for c in range(D_FF // FF_CHUNK):
        lo = c * FF_CHUNK
        gate = _dot(v, wgu_ref[:, lo:lo + FF_CHUNK])
        up = _dot(v, wgu_ref[:, D_FF + lo:D_FF + lo + FF_CHUNK])
        act_ref[:, lo:lo + FF_CHUNK] = (gate * jax.nn.sigmoid(gate) * up).astype(BF16)
    h2 = h1 + _dot(act_ref[...], wdn_ref[...])
    if last:
        h2 = h2 * _rms_scale(h2) * fg_ref[...]
    o_ref[0] = h2


def _out_call(h, mixap, f, layer, p, *, tile, last, meta=None):
    first = meta is not None
    assert not (first and last)
    B, _, D = h.shape
    L = mixap.shape[1]
    row = lambda b, i: (b, i, 0)
    const = lambda b, i: (layer, 0, 0)
    single = pl.Buffered(1)
    scratch = [pltpu.VMEM((tile, D_FF), BF16)]
    permuted_n2 = None
    if last:
        out_rows = L - N_META
        assert out_rows % tile == 0 and f.ndim == 3 and N_META % BF16_ROWS == 0 and tile % BF16_ROWS == 0
        nt = out_rows // tile

        def window(width, align):
            return pl.BlockSpec((pl.Element(1), pl.Element(tile), pl.Element(width)),
                                lambda b, i: (b, align * (N_META // align + i * (tile // align)), 0))

        tile_specs, tile_args = [window(D, SUBLANES)], (h,)
        ap_spec, f_spec = window(AP_WIDTH, BF16_ROWS), window(FOURIER_WIDTH, SUBLANES)
    else:
        out_rows = L
        nt = pl.cdiv(L, tile)
        if first:
            tile_specs, tile_args = _first_layer_specs(h, tile), (h, h, h, meta)
        else:
            tile_specs, tile_args = [pl.BlockSpec((1, tile, D), row)], (h,)
        ap_spec = pl.BlockSpec((1, tile, AP_WIDTH), row)
        if f.ndim == 4:
            permuted_n2 = f.shape[1]
            assert tile == SUBLANES * permuted_n2 and f.shape[2] * permuted_n2 == L
            f_spec = pl.BlockSpec((1, permuted_n2, SUBLANES, FOURIER_WIDTH), lambda b, i: (b, 0, i, 0))
            scratch.append(pltpu.VMEM((tile, FOURIER_WIDTH), F32))
        else:
            f_spec = pl.BlockSpec((1, tile, FOURIER_WIDTH), row)
    return pl.pallas_call(
        functools.partial(_out_kernel, last=last, first=first, permuted_n2=permuted_n2),
        grid=(B, nt),
        in_specs=tile_specs + [
            ap_spec,
            f_spec,
            pl.BlockSpec((None, 1, FOURIER_WIDTH), const),
            pl.BlockSpec((None, D, D), const, pipeline_mode=single),
            pl.BlockSpec((None, 1, D), const),
            pl.BlockSpec((None, D, 2 * D_FF), const, pipeline_mode=single),
            pl.BlockSpec((None, D_FF, D), const, pipeline_mode=single),
            pl.BlockSpec((1, D), lambda b, i: (0, 0)),
        ],
        out_specs=pl.BlockSpec((1, tile, D), row),
        out_shape=jax.ShapeDtypeStruct((B, out_rows, D), F32),
        scratch_shapes=scratch,
        compiler_params=pltpu.CompilerParams(
            dimension_semantics=("parallel", "parallel"), vmem_limit_bytes=VMEM_LIMIT_BYTES),
        name="out_proj_ffn",
    )(*tile_args, mixap, f, p["mix_g_f"], p["w_out"], p["norm2_g"], p["w_gate_up"], p["w_down"],
      p["final_g"])


def _trunk(x, meta_tokens, p, *, tile, last_tile, n1, n2, groups=BF16_ROWS, rows_b=None):
    B, S, D = x.shape
    L = S + N_META
    assert n1 * n2 == L and n1 % SUBLANES == 0 and tile % BF16_ROWS == 0
    assert rows_b is None or rows_b % (groups * n1) in (0, L % (groups * n1))
    meta = meta_tokens.astype(x.dtype)
    h = x
    tabs = _dft_tables(n1, n2, groups, l1_major=rows_b is None)
    p = dict(p, inv_cnt=_inverse_counts(L, tile, pl.cdiv(L, tile)))
    depth = p["w_in"].shape[0]
    for layer in range(depth):
        last = layer == depth - 1
        first = meta if layer == 0 else None
        mixap, xf = _in_call(h, layer, p, tile=tile, meta=first)
        if rows_b is None:
            f = _dft_rows_call(xf, tabs, n1=n1, n2=n2)
        else:
            f = _dft_call(xf, tabs, n1=n1, n2=n2, groups=groups, rows_b=rows_b, natural=last)
        h = _out_call(h, mixap, f, layer, p, tile=last_tile if last else tile, last=last, meta=first)
    return h


def _reordered_w_in(w_in):
    c0 = CONV_WIDTH
    xa, gb, gc, rest = w_in[..., :c0], w_in[..., c0:2 * c0], w_in[..., 2 * c0:3 * c0], w_in[..., 3 * c0:]
    return jnp.concatenate([xa, gc, rest[..., :POOL_WIDTH], gb, rest[..., POOL_WIDTH:]], axis=-1)


def kernel(x_prompt, x_sample, meta_tokens, norm1_g, w_in, conv_w, pool_w, pool_scale, mix_g,
           w_out, norm2_g, w_gate_up, w_down, final_g):
    depth = w_in.shape[0]
    pool_bd = jnp.zeros((depth, POOL_WIDTH, POOL_WIDTH), F32)
    for g in range(POOL_GROUPS):
        sl = slice(g * POOL_GROUP_DIM, (g + 1) * POOL_GROUP_DIM)
        pool_bd = pool_bd.at[:, sl, sl].set(pool_w[:, g])
    p = dict(
        norm1_g=norm1_g[:, None, :],
        w_in=_reordered_w_in(w_in).astype(BF16),
        conv_w=conv_w,
        pool_bd=pool_bd.astype(BF16),
        pool_scale=pool_scale[:, None, :],
        mix_g=mix_g[:, None, :],
        mix_g_f=mix_g[:, None, AP_WIDTH:],
        w_out=w_out.astype(BF16),
        norm2_g=norm2_g[:, None, :],
        w_gate_up=w_gate_up.astype(BF16),
        w_down=w_down.astype(BF16),
        final_g=final_g[None, :],
    )
    y_prompt = _trunk(x_prompt, meta_tokens, p, tile=656, last_tile=1024, n1=200, n2=82, groups=1, rows_b=2000)
    y_sample = _trunk(x_sample, meta_tokens, p, tile=688, last_tile=1024, n1=16, n2=257)
    return (y_prompt, y_sample)
```

```python
import functools

import numpy as np
import jax
import jax.numpy as jnp
from jax import lax
from jax.experimental import pallas as pl
from jax.experimental.pallas import tpu as pltpu

D_MODEL = 1024
N_META = 16
EPS = 1e-6
CONV_WIDTH = 384
POOL_WIDTH = 384
POOL_GROUPS = 4
POOL_GROUP_DIM = POOL_WIDTH // POOL_GROUPS
POOL_WINDOWS = (2, 4, 8, 16)
FOURIER_WIDTH = 256
FOURIER_HEAD_DIM = 64
AP_WIDTH = CONV_WIDTH + POOL_WIDTH
IN_WIDTH = 3 * CONV_WIDTH + POOL_WIDTH + FOURIER_WIDTH
D_FF = 2816
FF_CHUNK = 256

HALO = 8
SUBLANES = 8
LANES = 128
BF16_ROWS = 16
VMEM_LIMIT_BYTES = 56 * 1024 * 1024

F32 = jnp.float32
BF16 = jnp.bfloat16


def _rms_scale(x):
    return lax.rsqrt(jnp.mean(x * x, axis=-1, keepdims=True) + EPS)


def _dot(a, b):
    return jnp.dot(a, b, preferred_element_type=F32)


def _first_layer_specs(x, tile):
    B, S, D = x.shape
    e = N_META
    assert e == BF16_ROWS and tile % e == 0 and pl.cdiv(S + e, tile) * tile - 2 * e <= S
    te = tile // e
    return [
        pl.BlockSpec((1, e, D), lambda b, i: (b, jnp.maximum(i * te - 1, 0), 0)),
        pl.BlockSpec((pl.Element(1), pl.Element(tile - 2 * e), pl.Element(D)),
                     lambda b, i: (b, SUBLANES * (i * (tile // SUBLANES)), 0)),
        pl.BlockSpec((1, e, D), lambda b, i: (b, jnp.minimum((i + 1) * te - 2, S // e - 1), 0)),
        pl.BlockSpec((e, D), lambda b, i: (0, 0)),
    ]


def _first_layer_tile(head_ref, mid_ref, tail_ref, meta_ref):
    head = jnp.where(pl.program_id(1) == 0, meta_ref[...], head_ref[0])
    return jnp.concatenate([head, mid_ref[0], tail_ref[0]], axis=0)


def _in_kernel(*refs, tile, seq_len, first):
    n_tile_refs = 4 if first else 1
    hp_ref, hn_ref, g1_ref, win_ref, cw_ref, pbd_ref, ps_ref, mg_ref, icnt_ref, mixap_ref, xf_ref = (
        refs[n_tile_refs:])
    hm = _first_layer_tile(*refs[:4]) if first else refs[0][0]
    n = tile + 2 * HALO
    main = slice(HALO, HALO + tile)
    i = pl.program_id(1)
    is_last = i == pl.num_programs(1) - 1
    valid_in_last = seq_len - (pl.cdiv(seq_len, tile) - 1) * tile
    hp = jnp.where(i == 0, 0.0, hp_ref[0])
    hn = jnp.where(is_last, 0.0, hn_ref[0])
    if valid_in_last < tile:
        hm = jnp.concatenate([hm[:valid_in_last], jnp.where(is_last, 0.0, hm[valid_in_last:])], axis=0)
    hx = jnp.concatenate([hp, hm, hn], axis=0)
    u = (hx * _rms_scale(hx) * g1_ref[...]).astype(BF16)
    z = _dot(u, win_ref[...])
    c0 = CONV_WIDTH
    xa_cols, gc_cols, xp_cols, gb_cols = (slice(k * c0, (k + 1) * c0) for k in range(4))
    xf_ref[0] = z[main, 4 * c0:]
    ga = z[:, gc_cols] * z[:, xa_cols]
    xp = z[:, xp_cols]

    def shift(x, s):
        return pltpu.roll(x, s % n, axis=0)

    cw = cw_ref[...]
    conv = shift(ga, 1)[main] * cw[0:1] + ga[main] * cw[1:2] + shift(ga, -1)[main] * cw[2:3]
    a = z[main, gb_cols] * conv

    rights = [w - 1 - w // 2 for w in POOL_WINDOWS]
    lane = lax.broadcasted_iota(jnp.int32, (1, LANES), 1)
    cols = []
    for c in range(POOL_WIDTH // LANES):
        x = xp[:, c * LANES:(c + 1) * LANES]
        t2 = x + shift(x, 1)
        t4 = t2 + shift(t2, 2)
        if c == 0:
            lo, hi = t2, shift(t4, -rights[1])
        else:
            t8 = t4 + shift(t4, 4)
            if c == 1:
                lo, hi = shift(t4, -rights[1]), shift(t8, -rights[2])
            else:
                t8d = shift(t8, 1)
                lo, hi = shift(t8, -rights[2]), t8d + shift(t8d, -8)
        boundary = (c + 1) * POOL_GROUP_DIM - c * LANES
        cols.append(jnp.where(lane < boundary, lo[main], hi[main]))
    wsum = jnp.concatenate(cols, axis=-1)
    pm = wsum * icnt_ref[0] - xp[main]
    p = _dot(pm.astype(BF16), pbd_ref[...]) * ps_ref[...]

    mg = mg_ref[...]
    an = a * _rms_scale(a) * mg[:, :CONV_WIDTH]
    pn = p * _rms_scale(p) * mg[:, CONV_WIDTH:AP_WIDTH]
    mixap_ref[0] = jnp.concatenate([an, pn], axis=-1).astype(BF16)


def _inverse_counts(seq_len, tile, n_tiles):
    lane = lax.broadcasted_iota(jnp.int32, (1, 1, POOL_WIDTH), 2)
    left = jnp.zeros_like(lane)
    right = jnp.zeros_like(lane)
    for g, w in enumerate(POOL_WINDOWS):
        in_group = (lane // POOL_GROUP_DIM) == g
        left = jnp.where(in_group, w // 2, left)
        right = jnp.where(in_group, w - 1 - w // 2, right)
    first_row = jnp.asarray([0, tile, (n_tiles - 1) * tile], jnp.int32)[:, None, None]
    pos = first_row + lax.broadcasted_iota(jnp.int32, (1, tile, 1), 1)
    cnt = jnp.minimum(pos + right, seq_len - 1) - jnp.maximum(pos - left, 0) + 1
    return 1.0 / jnp.maximum(cnt, 1).astype(F32)


def _in_call(h, layer, p, *, tile, meta=None):
    first = meta is not None
    B, rows, D = h.shape
    L = rows + N_META if first else rows
    nt = pl.cdiv(L, tile)
    assert nt >= 3 and (nt - 1) * tile + HALO <= L and tile >= HALO
    tb = tile // SUBLANES
    last8 = rows // SUBLANES - 1
    shift8 = (L - rows) // SUBLANES
    const = lambda b, i: (layer, 0, 0)
    tile_kind = lambda b, i: (jnp.where(i == 0, 0, jnp.where(i == nt - 1, 2, 1)), 0, 0)
    if first:
        tile_specs, tile_args = _first_layer_specs(h, tile), (h, h, h, meta)
    else:
        tile_specs, tile_args = [pl.BlockSpec((1, tile, D), lambda b, i: (b, i, 0))], (h,)
    return pl.pallas_call(
        functools.partial(_in_kernel, tile=tile, seq_len=L, first=first),
        grid=(B, nt),
        in_specs=tile_specs + [
            pl.BlockSpec((1, HALO, D), lambda b, i: (b, jnp.maximum(i * tb - 1 - shift8, 0), 0)),
            pl.BlockSpec((1, HALO, D), lambda b, i: (b, jnp.minimum((i + 1) * tb - shift8, last8), 0)),
            pl.BlockSpec((None, 1, D), const),
            pl.BlockSpec((None, D, IN_WIDTH), const),
            pl.BlockSpec((None, 3, CONV_WIDTH), const),
            pl.BlockSpec((None, POOL_WIDTH, POOL_WIDTH), const),
            pl.BlockSpec((None, 1, POOL_WIDTH), const),
            pl.BlockSpec((None, 1, D), const),
            pl.BlockSpec((1, tile, POOL_WIDTH), tile_kind),
        ],
        out_specs=[
            pl.BlockSpec((1, tile, AP_WIDTH), lambda b, i: (b, i, 0)),
            pl.BlockSpec((1, tile, FOURIER_WIDTH), lambda b, i: (b, i, 0)),
        ],
        out_shape=[
            jax.ShapeDtypeStruct((B, L, AP_WIDTH), BF16),
            jax.ShapeDtypeStruct((B, L, FOURIER_WIDTH), F32),
        ],
        compiler_params=pltpu.CompilerParams(
            dimension_semantics=("parallel", "parallel"), vmem_limit_bytes=VMEM_LIMIT_BYTES),
        name="in_proj_mixers",
    )(*tile_args, h, h, p["norm1_g"], p["w_in"], p["conv_w"], p["pool_bd"], p["pool_scale"], p["mix_g"],
      p["inv_cnt"])


def _dft_a_kernel(x_ref, ma_ref, o_ref, xs_ref, *, n2, n2p):
    for j in range(SUBLANES):
        xs_ref[j] = x_ref[0, :, j, :]
    for j in range(SUBLANES):
        hj = _dot(ma_ref[j], xs_ref[j].astype(BF16))
        o_ref[0, 0, :, j, :] = hj[:n2]
        o_ref[0, 1, :, j, :] = hj[n2p:n2p + n2]


def _dft_b_kernel(h_ref, wb_ref, c64_ref, s64_ref, o_ref, *, rows, chunk, scale):
    c64, s64 = c64_ref[...], s64_ref[...]
    for r0 in range(0, rows, chunk):
        m = min(chunk, rows - r0)
        wc, ws = wb_ref[0, :m, :m], wb_ref[1, :m, :m]
        hr = h_ref[0, 0, r0:r0 + m, :].astype(BF16)
        hi = h_ref[0, 1, r0:r0 + m, :].astype(BF16)
        yr = _dot(wc, hr) + _dot(ws, hi)
        yi = _dot(wc, hi) - _dot(ws, hr)
        f = _dot(yr.astype(BF16), c64) + _dot(yi.astype(BF16), s64)
        o_ref[0, r0:r0 + m, :] = f * scale


def _unpermute_kernel(x_ref, o_ref, *, n2):
    for j in range(SUBLANES):
        o_ref[0, j * n2:(j + 1) * n2, :] = x_ref[0, :, j, :]


def _dft_rows_kernel(x_ref, ma_ref, wb_ref, c64_ref, s64_ref, o_ref, xs_ref, h_ref, *, n1, n2, n2p, scale):
    for j0 in range(0, n1, SUBLANES):
        for j in range(SUBLANES):
            xs_ref[j] = x_ref[0, :, j0 + j, :]
        for j in range(SUBLANES):
            h_ref[j0 + j] = _dot(ma_ref[j0 + j], xs_ref[j].astype(BF16)).astype(BF16)
    c64, s64 = c64_ref[...], s64_ref[...]
    wc, ws = wb_ref[0], wb_ref[1]
    run = BF16_ROWS
    for k0 in range(0, n2, run):
        keep = min(run, n2 - k0)
        hr = jnp.concatenate([h_ref[l1, k0:k0 + run, :] for l1 in range(n1)], axis=0)
        hi = jnp.concatenate([h_ref[l1, n2p + k0:n2p + k0 + run, :] for l1 in range(n1)], axis=0)
        yr = _dot(wc, hr) + _dot(ws, hi)
        yi = _dot(wc, hi) - _dot(ws, hr)
        f = (_dot(yr.astype(BF16), c64) + _dot(yi.astype(BF16), s64)) * scale
        for k1 in range(n1):
            o_ref[0, n2 * k1 + k0:n2 * k1 + k0 + keep, :] = f[run * k1:run * k1 + keep]


def _angle(idx, period):
    return (idx % period).astype(F32) * (2.0 * np.pi / period)


def _dft_tables(n1, n2, groups, l1_major):
    n = n1 * n2
    n2p = -(-n2 // BF16_ROWS) * BF16_ROWS
    l1, k2 = (lax.broadcasted_iota(jnp.int32, (n1, n2p, 1), d) for d in range(2))
    k2b, l2 = (lax.broadcasted_iota(jnp.int32, (1, n2p, n2), d) for d in (1, 2))
    a, b = _angle(k2 * l1, n), _angle(k2b * l2, n2)
    ca, sa, cb, sb = jnp.cos(a), jnp.sin(a), jnp.cos(b), jnp.sin(b)
    ma = jnp.concatenate([ca * cb - sa * sb, -(sa * cb + ca * sb)], axis=1).astype(BF16)
    g = groups * n1
    r, c = (lax.broadcasted_iota(jnp.int32, (g, g), d) for d in range(2))
    if l1_major:
        angb = _angle((r // groups) * (c // groups), n1)
        same = (r % groups) == (c % groups)
    else:
        angb = _angle((r % n1) * (c % n1), n1)
        same = (r // n1) == (c // n1)
    wb = jnp.stack([jnp.where(same, jnp.cos(angb), 0.0), jnp.where(same, jnp.sin(angb), 0.0)]).astype(BF16)
    d, e = (lax.broadcasted_iota(jnp.int32, (FOURIER_WIDTH, FOURIER_WIDTH), d) for d in range(2))
    same_head = (d // FOURIER_HEAD_DIM) == (e // FOURIER_HEAD_DIM)
    ang64 = _angle(d * e, FOURIER_HEAD_DIM)
    return dict(ma=ma, wb=wb,
                c64=jnp.where(same_head, jnp.cos(ang64), 0.0).astype(BF16),
                s64=jnp.where(same_head, jnp.sin(ang64), 0.0).astype(BF16))


def _dft_rows_call(xf, tabs, *, n1, n2):
    B, L, W = xf.shape
    n2p = tabs["ma"].shape[1] // 2
    g = BF16_ROWS * n1
    return pl.pallas_call(
        functools.partial(_dft_rows_kernel, n1=n1, n2=n2, n2p=n2p,
                          scale=float(1.0 / np.sqrt(L * FOURIER_HEAD_DIM))),
        grid=(B,),
        in_specs=[
            pl.BlockSpec((1, n2, n1, W), lambda b: (b, 0, 0, 0)),
            pl.BlockSpec((n1, 2 * n2p, n2), lambda b: (0, 0, 0)),
            pl.BlockSpec((2, g, g), lambda b: (0, 0, 0)),
            pl.BlockSpec((W, W), lambda b: (0, 0)),
            pl.BlockSpec((W, W), lambda b: (0, 0)),
        ],
        out_specs=pl.BlockSpec((1, L, W), lambda b: (b, 0, 0)),
        out_shape=jax.ShapeDtypeStruct((B, L, W), F32),
        scratch_shapes=[pltpu.VMEM((SUBLANES, n2, W), F32), pltpu.VMEM((n1, 2 * n2p, W), BF16)],
        compiler_params=pltpu.CompilerParams(
            dimension_semantics=("parallel",), vmem_limit_bytes=VMEM_LIMIT_BYTES),
        name="seq_dft_rows",
    )(xf.reshape(B, n2, n1, W), tabs["ma"], tabs["wb"], tabs["c64"], tabs["s64"])


def _dft_call(xf, tabs, *, n1, n2, groups, rows_b, natural):
    B, L, W = xf.shape
    n2p = tabs["ma"].shape[1] // 2
    chunk = groups * n1
    params = pltpu.CompilerParams(
        dimension_semantics=("parallel", "parallel"), vmem_limit_bytes=VMEM_LIMIT_BYTES)
    nblk = n1 // SUBLANES
    h = pl.pallas_call(
        functools.partial(_dft_a_kernel, n2=n2, n2p=n2p),
        grid=(B, nblk),
        in_specs=[
            pl.BlockSpec((1, n2, SUBLANES, W), lambda b, i: (b, 0, i, 0)),
            pl.BlockSpec((SUBLANES, 2 * n2p, n2), lambda b, i: (i, 0, 0)),
        ],
        out_specs=pl.BlockSpec((1, 2, n2, SUBLANES, W), lambda b, i: (b, 0, 0, i, 0)),
        out_shape=jax.ShapeDtypeStruct((B, 2, n2, n1, W), F32),
        scratch_shapes=[pltpu.VMEM((SUBLANES, n2, W), F32)],
        compiler_params=params,
        name="seq_dft_stage_a",
    )(xf.reshape(B, n2, n1, W), tabs["ma"])
    const2 = lambda b, i: (0, 0)
    f = pl.pallas_call(
        functools.partial(_dft_b_kernel, rows=rows_b, chunk=chunk,
                          scale=float(1.0 / np.sqrt(L * FOURIER_HEAD_DIM))),
        grid=(B, pl.cdiv(L, rows_b)),
        in_specs=[
            pl.BlockSpec((1, 2, rows_b, W), lambda b, i: (b, 0, i, 0)),
            pl.BlockSpec((2, chunk, chunk), lambda b, i: (0, 0, 0)),
            pl.BlockSpec((W, W), const2),
            pl.BlockSpec((W, W), const2),
        ],
        out_specs=pl.BlockSpec((1, rows_b, W), lambda b, i: (b, i, 0)),
        out_shape=jax.ShapeDtypeStruct((B, L, W), F32),
        compiler_params=params,
        name="seq_dft_stage_b",
    )(h.reshape(B, 2, L, W), tabs["wb"], tabs["c64"], tabs["s64"])
    f = f.reshape(B, n2, n1, W)
    if not natural:
        return f
    return pl.pallas_call(
        functools.partial(_unpermute_kernel, n2=n2),
        grid=(B, nblk),
        in_specs=[pl.BlockSpec((1, n2, SUBLANES, W), lambda b, i: (b, 0, i, 0))],
        out_specs=pl.BlockSpec((1, SUBLANES * n2, W), lambda b, i: (b, i, 0)),
        out_shape=jax.ShapeDtypeStruct((B, L, W), F32),
        compiler_params=params,
        name="seq_dft_unpermute",
    )(f)


def _out_kernel(*refs, last, first, permuted_n2):
    n_tile_refs = 4 if first else 1
    mixap_ref, f_ref, mgf_ref, wout_ref, g2_ref, wgu_ref, wdn_ref, fg_ref, o_ref, act_ref, *scratch = (
        refs[n_tile_refs:])
    h = _first_layer_tile(*refs[:4]) if first else refs[0][0]
    if permuted_n2 is None:
        f = f_ref[0]
    else:
        fbuf_ref, = scratch
        for j in range(SUBLANES):
            fbuf_ref[j * permuted_n2:(j + 1) * permuted_n2, :] = f_ref[0, :, j, :]
        f = fbuf_ref[...]
    fn = (f * _rms_scale(f) * mgf_ref[...]).astype(BF16)
    mix = jnp.concatenate([mixap_ref[0], fn], axis=-1)
    h1 = h + _dot(mix, wout_ref[...])
    v = (h1 * _rms_scale(h1) * g2_ref[...]).astype(BF16)
    for c in range(D_FF // FF_CHUNK):
        lo = c * FF_CHUNK
        gate = _dot(v, wgu_ref[:, lo:lo + FF_CHUNK])
        up = _dot(v, wgu_ref[:, D_FF + lo:D_FF + lo + FF_CHUNK])
        act_ref[:, lo:lo + FF_CHUNK] = (gate * jax.nn.sigmoid(gate) * up).astype(BF16)
    h2 = h1 + _dot(act_ref[...], wdn_ref[...])
    if last:
        h2 = h2 * _rms_scale(h2) * fg_ref[...]
    o_ref[0] = h2


def _out_call(h, mixap, f, layer, p, *, tile, last, meta=None):
    first = meta is not None
    assert not (first and last)
    B, _, D = h.shape
    L = mixap.shape[1]
    row = lambda b, i: (b, i, 0)
    const = lambda b, i: (layer, 0, 0)
    single = pl.Buffered(1)
    scratch = [pltpu.VMEM((tile, D_FF), BF16)]
    permuted_n2 = None
    if last:
        out_rows = L - N_META
        assert out_rows % tile == 0 and f.ndim == 3 and N_META % BF16_ROWS == 0 and tile % BF16_ROWS == 0
        nt = out_rows // tile

        def window(width, align):
            return pl.BlockSpec((pl.Element(1), pl.Element(tile), pl.Element(width)),
                                lambda b, i: (b, align * (N_META // align + i * (tile // align)), 0))

        tile_specs, tile_args = [window(D, SUBLANES)], (h,)
        ap_spec, f_spec = window(AP_WIDTH, BF16_ROWS), window(FOURIER_WIDTH, SUBLANES)
    else:
        out_rows = L
        nt = pl.cdiv(L, tile)
        if first:
            tile_specs, tile_args = _first_layer_specs(h, tile), (h, h, h, meta)
        else:
            tile_specs, tile_args = [pl.BlockSpec((1, tile, D), row)], (h,)
        ap_spec = pl.BlockSpec((1, tile, AP_WIDTH), row)
        if f.ndim == 4:
            permuted_n2 = f.shape[1]
            assert tile == SUBLANES * permuted_n2 and f.shape[2] * permuted_n2 == L
            f_spec = pl.BlockSpec((1, permuted_n2, SUBLANES, FOURIER_WIDTH), lambda b, i: (b, 0, i, 0))
            scratch.append(pltpu.VMEM((tile, FOURIER_WIDTH), F32))
        else:
            f_spec = pl.BlockSpec((1, tile, FOURIER_WIDTH), row)
    return pl.pallas_call(
        functools.partial(_out_kernel, last=last, first=first, permuted_n2=permuted_n2),
        grid=(B, nt),
        in_specs=tile_specs + [
            ap_spec,
            f_spec,
            pl.BlockSpec((None, 1, FOURIER_WIDTH), const),
            pl.BlockSpec((None, D, D), const, pipeline_mode=single),
            pl.BlockSpec((None, 1, D), const),
            pl.BlockSpec((None, D, 2 * D_FF), const, pipeline_mode=single),
            pl.BlockSpec((None, D_FF, D), const, pipeline_mode=single),
            pl.BlockSpec((1, D), lambda b, i: (0, 0)),
        ],
        out_specs=pl.BlockSpec((1, tile, D), row),
        out_shape=jax.ShapeDtypeStruct((B, out_rows, D), F32),
        scratch_shapes=scratch,
        compiler_params=pltpu.CompilerParams(
            dimension_semantics=("parallel", "parallel"), vmem_limit_bytes=VMEM_LIMIT_BYTES),
        name="out_proj_ffn",
    )(*tile_args, mixap, f, p["mix_g_f"], p["w_out"], p["norm2_g"], p["w_gate_up"], p["w_down"],
      p["final_g"])


def _trunk(x, meta_tokens, p, *, tile, last_tile, n1, n2, groups=BF16_ROWS, rows_b=None):
    B, S, D = x.shape
    L = S + N_META
    assert n1 * n2 == L and n1 % SUBLANES == 0 and tile % BF16_ROWS == 0
    assert rows_b is None or rows_b % (groups * n1) in (0, L % (groups * n1))
    meta = meta_tokens.astype(x.dtype)
    h = x
    tabs = _dft_tables(n1, n2, groups, l1_major=rows_b is None)
    p = dict(p, inv_cnt=_inverse_counts(L, tile, pl.cdiv(L, tile)))
    depth = p["w_in"].shape[0]
    for layer in range(depth):
        last = layer == depth - 1
        first = meta if layer == 0 else None
        mixap, xf = _in_call(h, layer, p, tile=tile, meta=first)
        if rows_b is None:
            f = _dft_rows_call(xf, tabs, n1=n1, n2=n2)
        else:
            f = _dft_call(xf, tabs, n1=n1, n2=n2, groups=groups, rows_b=rows_b, natural=last)
        h = _out_call(h, mixap, f, layer, p, tile=last_tile if last else tile, last=last, meta=first)
    return h


def _reordered_w_in(w_in):
    c0 = CONV_WIDTH
    xa, gb, gc, rest = w_in[..., :c0], w_in[..., c0:2 * c0], w_in[..., 2 * c0:3 * c0], w_in[..., 3 * c0:]
    return jnp.concatenate([xa, gc, rest[..., :POOL_WIDTH], gb, rest[..., POOL_WIDTH:]], axis=-1)


def kernel(x_prompt, x_sample, meta_tokens, norm1_g, w_in, conv_w, pool_w, pool_scale, mix_g,
           w_out, norm2_g, w_gate_up, w_down, final_g):
    depth = w_in.shape[0]
    pool_bd = jnp.zeros((depth, POOL_WIDTH, POOL_WIDTH), F32)
    for g in range(POOL_GROUPS):
        sl = slice(g * POOL_GROUP_DIM, (g + 1) * POOL_GROUP_DIM)
        pool_bd = pool_bd.at[:, sl, sl].set(pool_w[:, g])
    p = dict(
        norm1_g=norm1_g[:, None, :],
        w_in=_reordered_w_in(w_in).astype(BF16),
        conv_w=conv_w,
        pool_bd=pool_bd.astype(BF16),
        pool_scale=pool_scale[:, None, :],
        mix_g=mix_g[:, None, :],
        mix_g_f=mix_g[:, None, AP_WIDTH:],
        w_out=w_out.astype(BF16),
        norm2_g=norm2_g[:, None, :],
        w_gate_up=w_gate_up.astype(BF16),
        w_down=w_down.astype(BF16),
        final_g=final_g[None, :],
    )
    y_prompt = _trunk(x_prompt, meta_tokens, p, tile=656, last_tile=1024, n1=200, n2=82, groups=1, rows_b=2000)
    y_sample = _trunk(x_sample, meta_tokens, p, tile=688, last_tile=1024, n1=16, n2=257)
    return (y_prompt, y_sample)
```

```python
import functools

import numpy as np
import jax
import jax.numpy as jnp
from jax import lax
from jax.experimental import pallas as pl
from jax.experimental.pallas import tpu as pltpu

D_MODEL = 1024
N_META = 16
EPS = 1e-6
CONV_WIDTH = 384
POOL_WIDTH = 384
POOL_GROUPS = 4
POOL_GROUP_DIM = POOL_WIDTH // POOL_GROUPS
POOL_WINDOWS = (2, 4, 8, 16)
FOURIER_WIDTH = 256
FOURIER_HEAD_DIM = 64
AP_WIDTH = CONV_WIDTH + POOL_WIDTH
IN_WIDTH = 3 * CONV_WIDTH + POOL_WIDTH + FOURIER_WIDTH
D_FF = 2816
FF_CHUNK = 256

HALO = 8
SUBLANES = 8
LANES = 128
BF16_ROWS = 16
VMEM_LIMIT_BYTES = 56 * 1024 * 1024

F32 = jnp.float32
BF16 = jnp.bfloat16


def _rms_scale(x):
    return lax.rsqrt(jnp.mean(x * x, axis=-1, keepdims=True) + EPS)


def _dot(a, b):
    return jnp.dot(a, b, preferred_element_type=F32)


def _first_layer_specs(x, tile):
    B, S, D = x.shape
    e = N_META
    assert e == BF16_ROWS and tile % e == 0 and pl.cdiv(S + e, tile) * tile - 2 * e <= S
    te = tile // e
    return [
        pl.BlockSpec((1, e, D), lambda b, i: (b, jnp.maximum(i * te - 1, 0), 0)),
        pl.BlockSpec((pl.Element(1), pl.Element(tile - 2 * e), pl.Element(D)),
                     lambda b, i: (b, SUBLANES * (i * (tile // SUBLANES)), 0)),
        pl.BlockSpec((1, e, D), lambda b, i: (b, jnp.minimum((i + 1) * te - 2, S // e - 1), 0)),
        pl.BlockSpec((e, D), lambda b, i: (0, 0)),
    ]


def _first_layer_tile(head_ref, mid_ref, tail_ref, meta_ref):
    head = jnp.where(pl.program_id(1) == 0, meta_ref[...], head_ref[0])
    return jnp.concatenate([head, mid_ref[0], tail_ref[0]], axis=0)


def _in_kernel(*refs, tile, seq_len, first):
    n_tile_refs = 4 if first else 1
    hp_ref, hn_ref, g1_ref, win_ref, cw_ref, pbd_ref, ps_ref, mg_ref, icnt_ref, mixap_ref, xf_ref = (
        refs[n_tile_refs:])
    hm = _first_layer_tile(*refs[:4]) if first else refs[0][0]
    n = tile + 2 * HALO
    main = slice(HALO, HALO + tile)
    i = pl.program_id(1)
    is_last = i == pl.num_programs(1) - 1
    valid_in_last = seq_len - (pl.cdiv(seq_len, tile) - 1) * tile
    hp = jnp.where(i == 0, 0.0, hp_ref[0])
    hn = jnp.where(is_last, 0.0, hn_ref[0])
    if valid_in_last < tile:
        hm = jnp.concatenate([hm[:valid_in_last], jnp.where(is_last, 0.0, hm[valid_in_last:])], axis=0)
    hx = jnp.concatenate([hp, hm, hn], axis=0)
    u = (hx * _rms_scale(hx) * g1_ref[...]).astype(BF16)
    z = _dot(u, win_ref[...])
    c0 = CONV_WIDTH
    xa_cols, gc_cols, xp_cols, gb_cols = (slice(k * c0, (k + 1) * c0) for k in range(4))
    xf_ref[0] = z[main, 4 * c0:]
    ga = z[:, gc_cols] * z[:, xa_cols]
    xp = z[:, xp_cols]

    def shift(x, s):
        return pltpu.roll(x, s % n, axis=0)

    cw = cw_ref[...]
    conv = shift(ga, 1)[main] * cw[0:1] + ga[main] * cw[1:2] + shift(ga, -1)[main] * cw[2:3]
    a = z[main, gb_cols] * conv

    rights = [w - 1 - w // 2 for w in POOL_WINDOWS]
    lane = lax.broadcasted_iota(jnp.int32, (1, LANES), 1)
    cols = []
    for c in range(POOL_WIDTH // LANES):
        x = xp[:, c * LANES:(c + 1) * LANES]
        t2 = x + shift(x, 1)
        t4 = t2 + shift(t2, 2)
        if c == 0:
            lo, hi = t2, shift(t4, -rights[1])
        else:
            t8 = t4 + shift(t4, 4)
            if c == 1:
                lo, hi = shift(t4, -rights[1]), shift(t8, -rights[2])
            else:
                t8d = shift(t8, 1)
                lo, hi = shift(t8, -rights[2]), t8d + shift(t8d, -8)
        boundary = (c + 1) * POOL_GROUP_DIM - c * LANES
        cols.append(jnp.where(lane < boundary, lo[main], hi[main]))
    wsum = jnp.concatenate(cols, axis=-1)
    pm = wsum * icnt_ref[0] - xp[main]
    p = _dot(pm.astype(BF16), pbd_ref[...]) * ps_ref[...]

    mg = mg_ref[...]
    an = a * _rms_scale(a) * mg[:, :CONV_WIDTH]
    pn = p * _rms_scale(p) * mg[:, CONV_WIDTH:AP_WIDTH]
    mixap_ref[0] = jnp.concatenate([an, pn], axis=-1).astype(BF16)


def _inverse_counts(seq_len, tile, n_tiles):
    lane = lax.broadcasted_iota(jnp.int32, (1, 1, POOL_WIDTH), 2)
    left = jnp.zeros_like(lane)
    right = jnp.zeros_like(lane)
    for g, w in enumerate(POOL_WINDOWS):
        in_group = (lane // POOL_GROUP_DIM) == g
        left = jnp.where(in_group, w // 2, left)
        right = jnp.where(in_group, w - 1 - w // 2, right)
    first_row = jnp.asarray([0, tile, (n_tiles - 1) * tile], jnp.int32)[:, None, None]
    pos = first_row + lax.broadcasted_iota(jnp.int32, (1, tile, 1), 1)
    cnt = jnp.minimum(pos + right, seq_len - 1) - jnp.maximum(pos - left, 0) + 1
    return 1.0 / jnp.maximum(cnt, 1).astype(F32)


def _in_call(h, layer, p, inv_cnt, *, tile, meta=None):
    first = meta is not None
    B, rows, D = h.shape
    L = rows + N_META if first else rows
    nt = pl.cdiv(L, tile)
    assert nt >= 3 and (nt - 1) * tile + HALO <= L and tile >= HALO
    tb = tile // SUBLANES
    last8 = rows // SUBLANES - 1
    shift8 = (L - rows) // SUBLANES
    const = lambda b, i: (layer, 0, 0)
    tile_kind = lambda b, i: (jnp.where(i == 0, 0, jnp.where(i == nt - 1, 2, 1)), 0, 0)
    if first:
        tile_specs, tile_args = _first_layer_specs(h, tile), (h, h, h, meta)
    else:
        tile_specs, tile_args = [pl.BlockSpec((1, tile, D), lambda b, i: (b, i, 0))], (h,)
    return pl.pallas_call(
        functools.partial(_in_kernel, tile=tile, seq_len=L, first=first),
        grid=(B, nt),
        in_specs=tile_specs + [
            pl.BlockSpec((1, HALO, D), lambda b, i: (b, jnp.maximum(i * tb - 1 - shift8, 0), 0)),
            pl.BlockSpec((1, HALO, D), lambda b, i: (b, jnp.minimum((i + 1) * tb - shift8, last8), 0)),
            pl.BlockSpec((None, 1, D), const),
            pl.BlockSpec((None, D, IN_WIDTH), const),
            pl.BlockSpec((None, 3, CONV_WIDTH), const),
            pl.BlockSpec((None, POOL_WIDTH, POOL_WIDTH), const),
            pl.BlockSpec((None, 1, POOL_WIDTH), const),
            pl.BlockSpec((None, 1, D), const),
            pl.BlockSpec((1, tile, POOL_WIDTH), tile_kind),
        ],
        out_specs=[
            pl.BlockSpec((1, tile, AP_WIDTH), lambda b, i: (b, i, 0)),
            pl.BlockSpec((1, tile, FOURIER_WIDTH), lambda b, i: (b, i, 0)),
        ],
        out_shape=[
            jax.ShapeDtypeStruct((B, L, AP_WIDTH), BF16),
            jax.ShapeDtypeStruct((B, L, FOURIER_WIDTH), F32),
        ],
        compiler_params=pltpu.CompilerParams(
            dimension_semantics=("parallel", "parallel"), vmem_limit_bytes=VMEM_LIMIT_BYTES),
        name="in_proj_mixers",
    )(*tile_args, h, h, p["norm1_g"], p["w_in"], p["conv_w"], p["pool_bd"], p["pool_scale"], p["mix_g"],
      inv_cnt)


def _dft_a_kernel(x_ref, ma_ref, o_ref, xs_ref, *, n2, n2p):
    per_step = xs_ref.shape[0]
    for j in range(per_step):
        xs_ref[j] = x_ref[0, :, j, :]
    for j in range(per_step):
        hj = _dot(ma_ref[j], xs_ref[j].astype(BF16))
        o_ref[0, 0, :, j, :] = hj[:n2]
        o_ref[0, 1, :, j, :] = hj[n2p:n2p + n2]


def _dft_b_kernel(h_ref, wb_ref, c64_ref, s64_ref, o_ref, *, rows, chunk, scale):
    c64, s64 = c64_ref[...], s64_ref[...]
    for r0 in range(0, rows, chunk):
        m = min(chunk, rows - r0)
        wc, ws = wb_ref[0, :m, :m], wb_ref[1, :m, :m]
        hr = h_ref[0, 0, r0:r0 + m, :].astype(BF16)
        hi = h_ref[0, 1, r0:r0 + m, :].astype(BF16)
        yr = _dot(wc, hr) + _dot(ws, hi)
        yi = _dot(wc, hi) - _dot(ws, hr)
        f = _dot(yr.astype(BF16), c64) + _dot(yi.astype(BF16), s64)
        o_ref[0, r0:r0 + m, :] = f * scale


def _unpermute_kernel(x_ref, o_ref, *, n2):
    for j in range(SUBLANES):
        o_ref[0, j * n2:(j + 1) * n2, :] = x_ref[0, :, j, :]


def _dft_rows_kernel(x_ref, ma_ref, wb_ref, c64_ref, s64_ref, o_ref, xs_ref, h_ref, *, n1, n2, n2p, scale):
    for j0 in range(0, n1, SUBLANES):
        for j in range(SUBLANES):
            xs_ref[j] = x_ref[0, :, j0 + j, :]
        for j in range(SUBLANES):
            h_ref[j0 + j] = _dot(ma_ref[j0 + j], xs_ref[j].astype(BF16)).astype(BF16)
    c64, s64 = c64_ref[...], s64_ref[...]
    wc, ws = wb_ref[0], wb_ref[1]
    run = BF16_ROWS
    for k0 in range(0, n2, run):
        keep = min(run, n2 - k0)
        hr = jnp.concatenate([h_ref[l1, k0:k0 + run, :] for l1 in range(n1)], axis=0)
        hi = jnp.concatenate([h_ref[l1, n2p + k0:n2p + k0 + run, :] for l1 in range(n1)], axis=0)
        yr = _dot(wc, hr) + _dot(ws, hi)
        yi = _dot(wc, hi) - _dot(ws, hr)
        f = (_dot(yr.astype(BF16), c64) + _dot(yi.astype(BF16), s64)) * scale
        for k1 in range(n1):
            o_ref[0, n2 * k1 + k0:n2 * k1 + k0 + keep, :] = f[run * k1:run * k1 + keep]


def _angle(idx, period):
    return (idx % period).astype(F32) * (2.0 * np.pi / period)


def _dft_tables(n1, n2, groups, l1_major):
    n = n1 * n2
    n2p = -(-n2 // BF16_ROWS) * BF16_ROWS
    l1, k2 = (lax.broadcasted_iota(jnp.int32, (n1, n2p, 1), d) for d in range(2))
    k2b, l2 = (lax.broadcasted_iota(jnp.int32, (1, n2p, n2), d) for d in (1, 2))
    a, b = _angle(k2 * l1, n), _angle(k2b * l2, n2)
    ca, sa, cb, sb = jnp.cos(a), jnp.sin(a), jnp.cos(b), jnp.sin(b)
    ma = jnp.concatenate([ca * cb - sa * sb, -(sa * cb + ca * sb)], axis=1).astype(BF16)
    g = groups * n1
    r, c = (lax.broadcasted_iota(jnp.int32, (g, g), d) for d in range(2))
    if l1_major:
        angb = _angle((r // groups) * (c // groups), n1)
        same = (r % groups) == (c % groups)
    else:
        angb = _angle((r % n1) * (c % n1), n1)
        same = (r // n1) == (c // n1)
    wb = jnp.stack([jnp.where(same, jnp.cos(angb), 0.0), jnp.where(same, jnp.sin(angb), 0.0)]).astype(BF16)
    d, e = (lax.broadcasted_iota(jnp.int32, (FOURIER_WIDTH, FOURIER_WIDTH), d) for d in range(2))
    same_head = (d // FOURIER_HEAD_DIM) == (e // FOURIER_HEAD_DIM)
    ang64 = _angle(d * e, FOURIER_HEAD_DIM)
    return dict(ma=ma, wb=wb,
                c64=jnp.where(same_head, jnp.cos(ang64), 0.0).astype(BF16),
                s64=jnp.where(same_head, jnp.sin(ang64), 0.0).astype(BF16))


def _dft_rows_call(xf, tabs, *, n1, n2):
    B, L, W = xf.shape
    n2p = tabs["ma"].shape[1] // 2
    g = BF16_ROWS * n1
    return pl.pallas_call(
        functools.partial(_dft_rows_kernel, n1=n1, n2=n2, n2p=n2p,
                          scale=float(1.0 / np.sqrt(L * FOURIER_HEAD_DIM))),
        grid=(B,),
        in_specs=[
            pl.BlockSpec((1, n2, n1, W), lambda b: (b, 0, 0, 0)),
            pl.BlockSpec((n1, 2 * n2p, n2), lambda b: (0, 0, 0)),
            pl.BlockSpec((2, g, g), lambda b: (0, 0, 0)),
            pl.BlockSpec((W, W), lambda b: (0, 0)),
            pl.BlockSpec((W, W), lambda b: (0, 0)),
        ],
        out_specs=pl.BlockSpec((1, L, W), lambda b: (b, 0, 0)),
        out_shape=jax.ShapeDtypeStruct((B, L, W), F32),
        scratch_shapes=[pltpu.VMEM((SUBLANES, n2, W), F32), pltpu.VMEM((n1, 2 * n2p, W), BF16)],
        compiler_params=pltpu.CompilerParams(
            dimension_semantics=("parallel",), vmem_limit_bytes=VMEM_LIMIT_BYTES),
        name="seq_dft_rows",
    )(xf.reshape(B, n2, n1, W), tabs["ma"], tabs["wb"], tabs["c64"], tabs["s64"])


def _dft_call(xf, tabs, *, n1, n2, groups, rows_b, natural):
    B, L, W = xf.shape
    n2p = tabs["ma"].shape[1] // 2
    chunk = groups * n1
    params = pltpu.CompilerParams(
        dimension_semantics=("parallel", "parallel"), vmem_limit_bytes=VMEM_LIMIT_BYTES)
    nblk = n1 // SUBLANES
    a_rows = 2 * SUBLANES
    h = pl.pallas_call(
        functools.partial(_dft_a_kernel, n2=n2, n2p=n2p),
        grid=(B, pl.cdiv(n1, a_rows)),
        in_specs=[
            pl.BlockSpec((1, n2, a_rows, W), lambda b, i: (b, 0, i, 0)),
            pl.BlockSpec((a_rows, 2 * n2p, n2), lambda b, i: (i, 0, 0)),
        ],
        out_specs=pl.BlockSpec((1, 2, n2, a_rows, W), lambda b, i: (b, 0, 0, i, 0)),
        out_shape=jax.ShapeDtypeStruct((B, 2, n2, n1, W), F32),
        scratch_shapes=[pltpu.VMEM((a_rows, n2, W), F32)],
        compiler_params=params,
        name="seq_dft_stage_a",
    )(xf.reshape(B, n2, n1, W), tabs["ma"])
    const2 = lambda b, i: (0, 0)
    f = pl.pallas_call(
        functools.partial(_dft_b_kernel, rows=rows_b, chunk=chunk,
                          scale=float(1.0 / np.sqrt(L * FOURIER_HEAD_DIM))),
        grid=(B, pl.cdiv(L, rows_b)),
        in_specs=[
            pl.BlockSpec((1, 2, rows_b, W), lambda b, i: (b, 0, i, 0)),
            pl.BlockSpec((2, chunk, chunk), lambda b, i: (0, 0, 0)),
            pl.BlockSpec((W, W), const2),
            pl.BlockSpec((W, W), const2),
        ],
        out_specs=pl.BlockSpec((1, rows_b, W), lambda b, i: (b, i, 0)),
        out_shape=jax.ShapeDtypeStruct((B, L, W), F32),
        compiler_params=params,
        name="seq_dft_stage_b",
    )(h.reshape(B, 2, L, W), tabs["wb"], tabs["c64"], tabs["s64"])
    f = f.reshape(B, n2, n1, W)
    if not natural:
        return f
    return pl.pallas_call(
        functools.partial(_unpermute_kernel, n2=n2),
        grid=(B, nblk),
        in_specs=[pl.BlockSpec((1, n2, SUBLANES, W), lambda b, i: (b, 0, i, 0))],
        out_specs=pl.BlockSpec((1, SUBLANES * n2, W), lambda b, i: (b, i, 0)),
        out_shape=jax.ShapeDtypeStruct((B, L, W), F32),
        compiler_params=params,
        name="seq_dft_unpermute",
    )(f)


def _out_kernel(*refs, last, first, permuted_n2):
    n_tile_refs = 4 if first else 1
    mixap_ref, f_ref, mgf_ref, wout_ref, g2_ref, wgu_ref, wdn_ref, fg_ref, o_ref, act_ref, *scratch = (
        refs[n_tile_refs:])
    h = _first_layer_tile(*refs[:4]) if first else refs[0][0]
    if permuted_n2 is None:
        f = f_ref[0]
    else:
        fbuf_ref, = scratch
        for j in range(SUBLANES):
            fbuf_ref[j * permuted_n2:(j + 1) * permuted_n2, :] = f_ref[0, :, j, :]
        f = fbuf_ref[...]
    fn = (f * _rms_scale(f) * mgf_ref[...]).astype(BF16)
    mix = jnp.concatenate([mixap_ref[0], fn], axis=-1)
    h1 = h + _dot(mix, wout_ref[...])
    v = (h1 * _rms_scale(h1) * g2_ref[...]).astype(BF16)
    for c in range(D_FF // FF_CHUNK):
        lo = c * FF_CHUNK
        gate = _dot(v, wgu_ref[:, lo:lo + FF_CHUNK])
        up = _dot(v, wgu_ref[:, D_FF + lo:D_FF + lo + FF_CHUNK])
        act_ref[:, lo:lo + FF_CHUNK] = (gate * jax.nn.sigmoid(gate) * up).astype(BF16)
    h2 = h1 + _dot(act_ref[...], wdn_ref[...])
    if last:
        h2 = h2 * _rms_scale(h2) * fg_ref[...]
    o_ref[0] = h2


def _out_call(h, mixap, f, layer, p, *, tile, last, meta=None):
    first = meta is not None
    assert not (first and last)
    B, _, D = h.shape
    L = mixap.shape[1]
    row = lambda b, i: (b, i, 0)
    const = lambda b, i: (layer, 0, 0)
    single = pl.Buffered(1)
    scratch = [pltpu.VMEM((tile, D_FF), BF16)]
    permuted_n2 = None
    if last:
        out_rows = L - N_META
        assert out_rows % tile == 0 and f.ndim == 3 and N_META % BF16_ROWS == 0 and tile % BF16_ROWS == 0
        nt = out_rows // tile

        def window(width, align):
            return pl.BlockSpec((pl.Element(1), pl.Element(tile), pl.Element(width)),
                                lambda b, i: (b, align * (N_META // align + i * (tile // align)), 0))

        tile_specs, tile_args = [window(D, SUBLANES)], (h,)
        ap_spec, f_spec = window(AP_WIDTH, BF16_ROWS), window(FOURIER_WIDTH, SUBLANES)
    else:
        out_rows = L
        nt = pl.cdiv(L, tile)
        if first:
            tile_specs, tile_args = _first_layer_specs(h, tile), (h, h, h, meta)
        else:
            tile_specs, tile_args = [pl.BlockSpec((1, tile, D), row)], (h,)
        ap_spec = pl.BlockSpec((1, tile, AP_WIDTH), row)
        if f.ndim == 4:
            permuted_n2 = f.shape[1]
            assert tile == SUBLANES * permuted_n2 and f.shape[2] * permuted_n2 == L
            f_spec = pl.BlockSpec((1, permuted_n2, SUBLANES, FOURIER_WIDTH), lambda b, i: (b, 0, i, 0))
            scratch.append(pltpu.VMEM((tile, FOURIER_WIDTH), F32))
        else:
            f_spec = pl.BlockSpec((1, tile, FOURIER_WIDTH), row)
    return pl.pallas_call(
        functools.partial(_out_kernel, last=last, first=first, permuted_n2=permuted_n2),
        grid=(B, nt),
        in_specs=tile_specs + [
            ap_spec,
            f_spec,
            pl.BlockSpec((None, 1, FOURIER_WIDTH), const),
            pl.BlockSpec((None, D, D), const, pipeline_mode=single),
            pl.BlockSpec((None, 1, D), const),
            pl.BlockSpec((None, D, 2 * D_FF), const, pipeline_mode=single),
            pl.BlockSpec((None, D_FF, D), const, pipeline_mode=single),
            pl.BlockSpec((1, D), lambda b, i: (0, 0)),
        ],
        out_specs=pl.BlockSpec((1, tile, D), row),
        out_shape=jax.ShapeDtypeStruct((B, out_rows, D), F32),
        scratch_shapes=scratch,
        compiler_params=pltpu.CompilerParams(
            dimension_semantics=("parallel", "parallel"), vmem_limit_bytes=VMEM_LIMIT_BYTES),
        name="out_proj_ffn",
    )(*tile_args, mixap, f, p["mix_g_f"], p["w_out"], p["norm2_g"], p["w_gate_up"], p["w_down"],
      p["final_g"])


def _trunk(x, meta_tokens, p, *, tile, in_tile, last_tile, n1, n2, groups=BF16_ROWS, rows_b=None):
    B, S, D = x.shape
    L = S + N_META
    assert n1 * n2 == L and n1 % SUBLANES == 0 and tile % BF16_ROWS == 0
    assert rows_b is None or rows_b % (groups * n1) in (0, L % (groups * n1))
    meta = meta_tokens.astype(x.dtype)
    h = x
    tabs = _dft_tables(n1, n2, groups, l1_major=rows_b is None)
    inv_cnt = {t: _inverse_counts(L, t, pl.cdiv(L, t)) for t in (tile, in_tile)}
    depth = p["w_in"].shape[0]
    for layer in range(depth):
        last = layer == depth - 1
        first = meta if layer == 0 else None
        t_in = tile if layer == 0 else in_tile
        mixap, xf = _in_call(h, layer, p, inv_cnt[t_in], tile=t_in, meta=first)
        if rows_b is None:
            f = _dft_rows_call(xf, tabs, n1=n1, n2=n2)
        else:
            f = _dft_call(xf, tabs, n1=n1, n2=n2, groups=groups, rows_b=rows_b, natural=last)
        h = _out_call(h, mixap, f, layer, p, tile=last_tile if last else tile, last=last, meta=first)
    return h


def _reordered_w_in(w_in):
    c0 = CONV_WIDTH
    xa, gb, gc, rest = w_in[..., :c0], w_in[..., c0:2 * c0], w_in[..., 2 * c0:3 * c0], w_in[..., 3 * c0:]
    return jnp.concatenate([xa, gc, rest[..., :POOL_WIDTH], gb, rest[..., POOL_WIDTH:]], axis=-1)


def kernel(x_prompt, x_sample, meta_tokens, norm1_g, w_in, conv_w, pool_w, pool_scale, mix_g,
           w_out, norm2_g, w_gate_up, w_down, final_g):
    depth = w_in.shape[0]
    pool_bd = jnp.zeros((depth, POOL_WIDTH, POOL_WIDTH), F32)
    for g in range(POOL_GROUPS):
        sl = slice(g * POOL_GROUP_DIM, (g + 1) * POOL_GROUP_DIM)
        pool_bd = pool_bd.at[:, sl, sl].set(pool_w[:, g])
    p = dict(
        norm1_g=norm1_g[:, None, :],
        w_in=_reordered_w_in(w_in).astype(BF16),
        conv_w=conv_w,
        pool_bd=pool_bd.astype(BF16),
        pool_scale=pool_scale[:, None, :],
        mix_g=mix_g[:, None, :],
        mix_g_f=mix_g[:, None, AP_WIDTH:],
        w_out=w_out.astype(BF16),
        norm2_g=norm2_g[:, None, :],
        w_gate_up=w_gate_up.astype(BF16),
        w_down=w_down.astype(BF16),
        final_g=final_g[None, :],
    )
    y_prompt = _trunk(x_prompt, meta_tokens, p, tile=656, in_tile=1040, last_tile=1024, n1=200, n2=82, groups=1, rows_b=2000)
    y_sample = _trunk(x_sample, meta_tokens, p, tile=688, in_tile=1040, last_tile=1024, n1=16, n2=257)
    return (y_prompt, y_sample)
```

```python
import functools

import numpy as np
import jax
import jax.numpy as jnp
from jax import lax
from jax.experimental import pallas as pl
from jax.experimental.pallas import tpu as pltpu

D_MODEL = 1024
N_META = 16
EPS = 1e-6
CONV_WIDTH = 384
POOL_WIDTH = 384
POOL_GROUPS = 4
POOL_GROUP_DIM = POOL_WIDTH // POOL_GROUPS
POOL_WINDOWS = (2, 4, 8, 16)
FOURIER_WIDTH = 256
FOURIER_HEAD_DIM = 64
AP_WIDTH = CONV_WIDTH + POOL_WIDTH
IN_WIDTH = 3 * CONV_WIDTH + POOL_WIDTH + FOURIER_WIDTH
D_FF = 2816
FF_CHUNK = 256

HALO = 8
SUBLANES = 8
LANES = 128
BF16_ROWS = 16
VMEM_LIMIT_BYTES = 56 * 1024 * 1024

F32 = jnp.float32
BF16 = jnp.bfloat16


def _rms_scale(x):
    return lax.rsqrt(jnp.mean(x * x, axis=-1, keepdims=True) + EPS)


def _dot(a, b):
    return jnp.dot(a, b, preferred_element_type=F32)


def _first_layer_specs(x, tile):
    B, S, D = x.shape
    e = N_META
    assert e == BF16_ROWS and tile % e == 0 and pl.cdiv(S + e, tile) * tile - 2 * e <= S
    te = tile // e
    return [
        pl.BlockSpec((1, e, D), lambda b, i: (b, jnp.maximum(i * te - 1, 0), 0)),
        pl.BlockSpec((pl.Element(1), pl.Element(tile - 2 * e), pl.Element(D)),
                     lambda b, i: (b, SUBLANES * (i * (tile // SUBLANES)), 0)),
        pl.BlockSpec((1, e, D), lambda b, i: (b, jnp.minimum((i + 1) * te - 2, S // e - 1), 0)),
        pl.BlockSpec((e, D), lambda b, i: (0, 0)),
    ]


def _first_layer_tile(head_ref, mid_ref, tail_ref, meta_ref):
    head = jnp.where(pl.program_id(1) == 0, meta_ref[...], head_ref[0])
    return jnp.concatenate([head, mid_ref[0], tail_ref[0]], axis=0)


def _in_kernel(*refs, tile, seq_len, first):
    n_tile_refs = 4 if first else 1
    hp_ref, hn_ref, g1_ref, win_ref, cw_ref, pbd_ref, ps_ref, mg_ref, icnt_ref, mixap_ref, xf_ref = (
        refs[n_tile_refs:])
    hm = _first_layer_tile(*refs[:4]) if first else refs[0][0]
    n = tile + 2 * HALO
    main = slice(HALO, HALO + tile)
    i = pl.program_id(1)
    is_last = i == pl.num_programs(1) - 1
    valid_in_last = seq_len - (pl.cdiv(seq_len, tile) - 1) * tile
    hp = jnp.where(i == 0, 0.0, hp_ref[0])
    hn = jnp.where(is_last, 0.0, hn_ref[0])
    if valid_in_last < tile:
        hm = jnp.concatenate([hm[:valid_in_last], jnp.where(is_last, 0.0, hm[valid_in_last:])], axis=0)
    hx = jnp.concatenate([hp, hm, hn], axis=0)
    u = (hx * _rms_scale(hx) * g1_ref[...]).astype(BF16)
    z = _dot(u, win_ref[...])
    c0 = CONV_WIDTH
    xa_cols, gc_cols, xp_cols, gb_cols = (slice(k * c0, (k + 1) * c0) for k in range(4))
    xf_ref[0] = z[main, 4 * c0:]
    ga = z[:, gc_cols] * z[:, xa_cols]
    xp = z[:, xp_cols]

    def shift(x, s):
        return pltpu.roll(x, s % n, axis=0)

    cw = cw_ref[...]
    conv = shift(ga, 1)[main] * cw[0:1] + ga[main] * cw[1:2] + shift(ga, -1)[main] * cw[2:3]
    a = z[main, gb_cols] * conv

    rights = [w - 1 - w // 2 for w in POOL_WINDOWS]
    lane = lax.broadcasted_iota(jnp.int32, (1, LANES), 1)
    cols = []
    for c in range(POOL_WIDTH // LANES):
        x = xp[:, c * LANES:(c + 1) * LANES]
        t2 = x + shift(x, 1)
        t4 = t2 + shift(t2, 2)
        if c == 0:
            lo, hi = t2, shift(t4, -rights[1])
        else:
            t8 = t4 + shift(t4, 4)
            if c == 1:
                lo, hi = shift(t4, -rights[1]), shift(t8, -rights[2])
            else:
                t8d = shift(t8, 1)
                lo, hi = shift(t8, -rights[2]), t8d + shift(t8d, -8)
        boundary = (c + 1) * POOL_GROUP_DIM - c * LANES
        cols.append(jnp.where(lane < boundary, lo[main], hi[main]))
    wsum = jnp.concatenate(cols, axis=-1)
    pm = wsum * icnt_ref[0] - xp[main]
    p = _dot(pm.astype(BF16), pbd_ref[...]) * ps_ref[...]

    mg = mg_ref[...]
    an = a * _rms_scale(a) * mg[:, :CONV_WIDTH]
    pn = p * _rms_scale(p) * mg[:, CONV_WIDTH:AP_WIDTH]
    mixap_ref[0] = jnp.concatenate([an, pn], axis=-1).astype(BF16)


def _inverse_counts(seq_len, tile, n_tiles):
    lane = lax.broadcasted_iota(jnp.int32, (1, 1, POOL_WIDTH), 2)
    left = jnp.zeros_like(lane)
    right = jnp.zeros_like(lane)
    for g, w in enumerate(POOL_WINDOWS):
        in_group = (lane // POOL_GROUP_DIM) == g
        left = jnp.where(in_group, w // 2, left)
        right = jnp.where(in_group, w - 1 - w // 2, right)
    first_row = jnp.asarray([0, tile, (n_tiles - 1) * tile], jnp.int32)[:, None, None]
    pos = first_row + lax.broadcasted_iota(jnp.int32, (1, tile, 1), 1)
    cnt = jnp.minimum(pos + right, seq_len - 1) - jnp.maximum(pos - left, 0) + 1
    return 1.0 / jnp.maximum(cnt, 1).astype(F32)


def _in_call(h, layer, p, inv_cnt, *, tile, meta=None):
    first = meta is not None
    B, rows, D = h.shape
    L = rows + N_META if first else rows
    nt = pl.cdiv(L, tile)
    assert nt >= 3 and (nt - 1) * tile + HALO <= L and tile >= HALO
    tb = tile // SUBLANES
    last8 = rows // SUBLANES - 1
    shift8 = (L - rows) // SUBLANES
    const = lambda b, i: (layer, 0, 0)
    tile_kind = lambda b, i: (jnp.where(i == 0, 0, jnp.where(i == nt - 1, 2, 1)), 0, 0)
    if first:
        tile_specs, tile_args = _first_layer_specs(h, tile), (h, h, h, meta)
    else:
        tile_specs, tile_args = [pl.BlockSpec((1, tile, D), lambda b, i: (b, i, 0))], (h,)
    return pl.pallas_call(
        functools.partial(_in_kernel, tile=tile, seq_len=L, first=first),
        grid=(B, nt),
        in_specs=tile_specs + [
            pl.BlockSpec((1, HALO, D), lambda b, i: (b, jnp.maximum(i * tb - 1 - shift8, 0), 0)),
            pl.BlockSpec((1, HALO, D), lambda b, i: (b, jnp.minimum((i + 1) * tb - shift8, last8), 0)),
            pl.BlockSpec((None, 1, D), const),
            pl.BlockSpec((None, D, IN_WIDTH), const),
            pl.BlockSpec((None, 3, CONV_WIDTH), const),
            pl.BlockSpec((None, POOL_WIDTH, POOL_WIDTH), const),
            pl.BlockSpec((None, 1, POOL_WIDTH), const),
            pl.BlockSpec((None, 1, D), const),
            pl.BlockSpec((1, tile, POOL_WIDTH), tile_kind),
        ],
        out_specs=[
            pl.BlockSpec((1, tile, AP_WIDTH), lambda b, i: (b, i, 0)),
            pl.BlockSpec((1, tile, FOURIER_WIDTH), lambda b, i: (b, i, 0)),
        ],
        out_shape=[
            jax.ShapeDtypeStruct((B, L, AP_WIDTH), BF16),
            jax.ShapeDtypeStruct((B, L, FOURIER_WIDTH), F32),
        ],
        compiler_params=pltpu.CompilerParams(
            dimension_semantics=("parallel", "parallel"), vmem_limit_bytes=VMEM_LIMIT_BYTES),
        name="in_proj_mixers",
    )(*tile_args, h, h, p["norm1_g"], p["w_in"], p["conv_w"], p["pool_bd"], p["pool_scale"], p["mix_g"],
      inv_cnt)


def _dft_a_kernel(x_ref, ma_ref, o_ref, xs_ref, *, n2, n2p):
    per_step = xs_ref.shape[0]
    for j in range(per_step):
        xs_ref[j] = x_ref[0, :, j, :]
    for j in range(per_step):
        hj = _dot(ma_ref[j], xs_ref[j].astype(BF16))
        o_ref[0, 0, :, j, :] = hj[:n2]
        o_ref[0, 1, :, j, :] = hj[n2p:n2p + n2]


def _dft_b_kernel(h_ref, wb_ref, c64_ref, s64_ref, o_ref, *, rows, chunk, scale):
    c64, s64 = c64_ref[...], s64_ref[...]
    for r0 in range(0, rows, chunk):
        m = min(chunk, rows - r0)
        wc, ws = wb_ref[0, :m, :m], wb_ref[1, :m, :m]
        hr = h_ref[0, 0, r0:r0 + m, :].astype(BF16)
        hi = h_ref[0, 1, r0:r0 + m, :].astype(BF16)
        yr = _dot(wc, hr) + _dot(ws, hi)
        yi = _dot(wc, hi) - _dot(ws, hr)
        f = _dot(yr.astype(BF16), c64) + _dot(yi.astype(BF16), s64)
        o_ref[0, r0:r0 + m, :] = f * scale


def _unpermute_kernel(x_ref, o_ref, *, n2):
    for j in range(SUBLANES):
        o_ref[0, j * n2:(j + 1) * n2, :] = x_ref[0, :, j, :]


def _dft_rows_kernel(x_ref, ma_ref, wb_ref, c64_ref, s64_ref, o_ref, xs_ref, h_ref, *, n1, n2, n2p, scale):
    for j0 in range(0, n1, SUBLANES):
        for j in range(SUBLANES):
            xs_ref[j] = x_ref[0, :, j0 + j, :]
        for j in range(SUBLANES):
            h_ref[j0 + j] = _dot(ma_ref[j0 + j], xs_ref[j].astype(BF16)).astype(BF16)
    c64, s64 = c64_ref[...], s64_ref[...]
    wc, ws = wb_ref[0], wb_ref[1]
    run = BF16_ROWS
    for k0 in range(0, n2, run):
        keep = min(run, n2 - k0)
        hr = jnp.concatenate([h_ref[l1, k0:k0 + run, :] for l1 in range(n1)], axis=0)
        hi = jnp.concatenate([h_ref[l1, n2p + k0:n2p + k0 + run, :] for l1 in range(n1)], axis=0)
        yr = _dot(wc, hr) + _dot(ws, hi)
        yi = _dot(wc, hi) - _dot(ws, hr)
        f = (_dot(yr.astype(BF16), c64) + _dot(yi.astype(BF16), s64)) * scale
        for k1 in range(n1):
            o_ref[0, n2 * k1 + k0:n2 * k1 + k0 + keep, :] = f[run * k1:run * k1 + keep]


def _angle(idx, period):
    return (idx % period).astype(F32) * (2.0 * np.pi / period)


def _dft_tables(n1, n2, groups, l1_major):
    n = n1 * n2
    n2p = -(-n2 // BF16_ROWS) * BF16_ROWS
    l1, k2 = (lax.broadcasted_iota(jnp.int32, (n1, n2p, 1), d) for d in range(2))
    k2b, l2 = (lax.broadcasted_iota(jnp.int32, (1, n2p, n2), d) for d in (1, 2))
    a, b = _angle(k2 * l1, n), _angle(k2b * l2, n2)
    ca, sa, cb, sb = jnp.cos(a), jnp.sin(a), jnp.cos(b), jnp.sin(b)
    ma = jnp.concatenate([ca * cb - sa * sb, -(sa * cb + ca * sb)], axis=1).astype(BF16)
    g = groups * n1
    r, c = (lax.broadcasted_iota(jnp.int32, (g, g), d) for d in range(2))
    if l1_major:
        angb = _angle((r // groups) * (c // groups), n1)
        same = (r % groups) == (c % groups)
    else:
        angb = _angle((r % n1) * (c % n1), n1)
        same = (r // n1) == (c // n1)
    wb = jnp.stack([jnp.where(same, jnp.cos(angb), 0.0), jnp.where(same, jnp.sin(angb), 0.0)]).astype(BF16)
    d, e = (lax.broadcasted_iota(jnp.int32, (FOURIER_WIDTH, FOURIER_WIDTH), d) for d in range(2))
    same_head = (d // FOURIER_HEAD_DIM) == (e // FOURIER_HEAD_DIM)
    ang64 = _angle(d * e, FOURIER_HEAD_DIM)
    return dict(ma=ma, wb=wb,
                c64=jnp.where(same_head, jnp.cos(ang64), 0.0).astype(BF16),
                s64=jnp.where(same_head, jnp.sin(ang64), 0.0).astype(BF16))


def _dft_rows_call(xf, tabs, *, n1, n2):
    B, L, W = xf.shape
    n2p = tabs["ma"].shape[1] // 2
    g = BF16_ROWS * n1
    return pl.pallas_call(
        functools.partial(_dft_rows_kernel, n1=n1, n2=n2, n2p=n2p,
                          scale=float(1.0 / np.sqrt(L * FOURIER_HEAD_DIM))),
        grid=(B,),
        in_specs=[
            pl.BlockSpec((1, n2, n1, W), lambda b: (b, 0, 0, 0)),
            pl.BlockSpec((n1, 2 * n2p, n2), lambda b: (0, 0, 0)),
            pl.BlockSpec((2, g, g), lambda b: (0, 0, 0)),
            pl.BlockSpec((W, W), lambda b: (0, 0)),
            pl.BlockSpec((W, W), lambda b: (0, 0)),
        ],
        out_specs=pl.BlockSpec((1, L, W), lambda b: (b, 0, 0)),
        out_shape=jax.ShapeDtypeStruct((B, L, W), F32),
        scratch_shapes=[pltpu.VMEM((SUBLANES, n2, W), F32), pltpu.VMEM((n1, 2 * n2p, W), BF16)],
        compiler_params=pltpu.CompilerParams(
            dimension_semantics=("parallel",), vmem_limit_bytes=VMEM_LIMIT_BYTES),
        name="seq_dft_rows",
    )(xf.reshape(B, n2, n1, W), tabs["ma"], tabs["wb"], tabs["c64"], tabs["s64"])


def _dft_call(xf, tabs, *, n1, n2, groups, rows_b, natural):
    B, L, W = xf.shape
    n2p = tabs["ma"].shape[1] // 2
    chunk = groups * n1
    params = pltpu.CompilerParams(
        dimension_semantics=("parallel", "parallel"), vmem_limit_bytes=VMEM_LIMIT_BYTES)
    nblk = n1 // SUBLANES
    a_rows = max(r for r in range(SUBLANES, 6 * SUBLANES, SUBLANES) if n1 % r == 0)
    h = pl.pallas_call(
        functools.partial(_dft_a_kernel, n2=n2, n2p=n2p),
        grid=(B, pl.cdiv(n1, a_rows)),
        in_specs=[
            pl.BlockSpec((1, n2, a_rows, W), lambda b, i: (b, 0, i, 0)),
            pl.BlockSpec((a_rows, 2 * n2p, n2), lambda b, i: (i, 0, 0)),
        ],
        out_specs=pl.BlockSpec((1, 2, n2, a_rows, W), lambda b, i: (b, 0, 0, i, 0)),
        out_shape=jax.ShapeDtypeStruct((B, 2, n2, n1, W), F32),
        scratch_shapes=[pltpu.VMEM((a_rows, n2, W), F32)],
        compiler_params=params,
        name="seq_dft_stage_a",
    )(xf.reshape(B, n2, n1, W), tabs["ma"])
    const2 = lambda b, i: (0, 0)
    f = pl.pallas_call(
        functools.partial(_dft_b_kernel, rows=rows_b, chunk=chunk,
                          scale=float(1.0 / np.sqrt(L * FOURIER_HEAD_DIM))),
        grid=(B, pl.cdiv(L, rows_b)),
        in_specs=[
            pl.BlockSpec((1, 2, rows_b, W), lambda b, i: (b, 0, i, 0)),
            pl.BlockSpec((2, chunk, chunk), lambda b, i: (0, 0, 0)),
            pl.BlockSpec((W, W), const2),
            pl.BlockSpec((W, W), const2),
        ],
        out_specs=pl.BlockSpec((1, rows_b, W), lambda b, i: (b, i, 0)),
        out_shape=jax.ShapeDtypeStruct((B, L, W), F32),
        compiler_params=params,
        name="seq_dft_stage_b",
    )(h.reshape(B, 2, L, W), tabs["wb"], tabs["c64"], tabs["s64"])
    f = f.reshape(B, n2, n1, W)
    if not natural:
        return f
    return pl.pallas_call(
        functools.partial(_unpermute_kernel, n2=n2),
        grid=(B, nblk),
        in_specs=[pl.BlockSpec((1, n2, SUBLANES, W), lambda b, i: (b, 0, i, 0))],
        out_specs=pl.BlockSpec((1, SUBLANES * n2, W), lambda b, i: (b, i, 0)),
        out_shape=jax.ShapeDtypeStruct((B, L, W), F32),
        compiler_params=params,
        name="seq_dft_unpermute",
    )(f)


def _out_kernel(*refs, last, first, permuted_n2):
    n_tile_refs = 4 if first else 1
    mixap_ref, f_ref, mgf_ref, wout_ref, g2_ref, wgu_ref, wdn_ref, fg_ref, o_ref, act_ref, *scratch = (
        refs[n_tile_refs:])
    h = _first_layer_tile(*refs[:4]) if first else refs[0][0]
    if permuted_n2 is None:
        f = f_ref[0]
    else:
        fbuf_ref, = scratch
        for j in range(SUBLANES):
            fbuf_ref[j * permuted_n2:(j + 1) * permuted_n2, :] = f_ref[0, :, j, :]
        f = fbuf_ref[...]
    fn = (f * _rms_scale(f) * mgf_ref[...]).astype(BF16)
    mix = jnp.concatenate([mixap_ref[0], fn], axis=-1)
    h1 = h + _dot(mix, wout_ref[...])
    v = (h1 * _rms_scale(h1) * g2_ref[...]).astype(BF16)
    for c in range(D_FF // FF_CHUNK):
        lo = c * FF_CHUNK
        gate = _dot(v, wgu_ref[:, lo:lo + FF_CHUNK])
        up = _dot(v, wgu_ref[:, D_FF + lo:D_FF + lo + FF_CHUNK])
        act_ref[:, lo:lo + FF_CHUNK] = (gate * jax.nn.sigmoid(gate) * up).astype(BF16)
    h2 = h1 + _dot(act_ref[...], wdn_ref[...])
    if last:
        h2 = h2 * _rms_scale(h2) * fg_ref[...]
    o_ref[0] = h2


def _out_call(h, mixap, f, layer, p, *, tile, last, meta=None):
    first = meta is not None
    assert not (first and last)
    B, _, D = h.shape
    L = mixap.shape[1]
    row = lambda b, i: (b, i, 0)
    const = lambda b, i: (layer, 0, 0)
    single = pl.Buffered(1)
    scratch = [pltpu.VMEM((tile, D_FF), BF16)]
    permuted_n2 = None
    if last:
        out_rows = L - N_META
        assert out_rows % tile == 0 and f.ndim == 3 and N_META % BF16_ROWS == 0 and tile % BF16_ROWS == 0
        nt = out_rows // tile

        def window(width, align):
            return pl.BlockSpec((pl.Element(1), pl.Element(tile), pl.Element(width)),
                                lambda b, i: (b, align * (N_META // align + i * (tile // align)), 0))

        tile_specs, tile_args = [window(D, SUBLANES)], (h,)
        ap_spec, f_spec = window(AP_WIDTH, BF16_ROWS), window(FOURIER_WIDTH, SUBLANES)
    else:
        out_rows = L
        nt = pl.cdiv(L, tile)
        if first:
            tile_specs, tile_args = _first_layer_specs(h, tile), (h, h, h, meta)
        else:
            tile_specs, tile_args = [pl.BlockSpec((1, tile, D), row)], (h,)
        ap_spec = pl.BlockSpec((1, tile, AP_WIDTH), row)
        if f.ndim == 4:
            permuted_n2 = f.shape[1]
            assert tile == SUBLANES * permuted_n2 and f.shape[2] * permuted_n2 == L
            f_spec = pl.BlockSpec((1, permuted_n2, SUBLANES, FOURIER_WIDTH), lambda b, i: (b, 0, i, 0))
            scratch.append(pltpu.VMEM((tile, FOURIER_WIDTH), F32))
        else:
            f_spec = pl.BlockSpec((1, tile, FOURIER_WIDTH), row)
    return pl.pallas_call(
        functools.partial(_out_kernel, last=last, first=first, permuted_n2=permuted_n2),
        grid=(B, nt),
        in_specs=tile_specs + [
            ap_spec,
            f_spec,
            pl.BlockSpec((None, 1, FOURIER_WIDTH), const),
            pl.BlockSpec((None, D, D), const, pipeline_mode=single),
            pl.BlockSpec((None, 1, D), const),
            pl.BlockSpec((None, D, 2 * D_FF), const, pipeline_mode=single),
            pl.BlockSpec((None, D_FF, D), const, pipeline_mode=single),
            pl.BlockSpec((1, D), lambda b, i: (0, 0)),
        ],
        out_specs=pl.BlockSpec((1, tile, D), row),
        out_shape=jax.ShapeDtypeStruct((B, out_rows, D), F32),
        scratch_shapes=scratch,
        compiler_params=pltpu.CompilerParams(
            dimension_semantics=("parallel", "parallel"), vmem_limit_bytes=VMEM_LIMIT_BYTES),
        name="out_proj_ffn",
    )(*tile_args, mixap, f, p["mix_g_f"], p["w_out"], p["norm2_g"], p["w_gate_up"], p["w_down"],
      p["final_g"])


def _trunk(x, meta_tokens, p, *, tile, in_tiles, last_tile, n1, n2, groups=BF16_ROWS, rows_b=None):
    B, S, D = x.shape
    L = S + N_META
    assert n1 * n2 == L and n1 % SUBLANES == 0 and tile % BF16_ROWS == 0
    assert rows_b is None or rows_b % (groups * n1) in (0, L % (groups * n1))
    meta = meta_tokens.astype(x.dtype)
    h = x
    tabs = _dft_tables(n1, n2, groups, l1_major=rows_b is None)
    inv_cnt = {t: _inverse_counts(L, t, pl.cdiv(L, t)) for t in in_tiles}
    depth = p["w_in"].shape[0]
    for layer in range(depth):
        last = layer == depth - 1
        first = meta if layer == 0 else None
        t_in = in_tiles[min(layer, 1)]
        mixap, xf = _in_call(h, layer, p, inv_cnt[t_in], tile=t_in, meta=first)
        if rows_b is None:
            f = _dft_rows_call(xf, tabs, n1=n1, n2=n2)
        else:
            f = _dft_call(xf, tabs, n1=n1, n2=n2, groups=groups, rows_b=rows_b, natural=last)
        h = _out_call(h, mixap, f, layer, p, tile=last_tile if last else tile, last=last, meta=first)
    return h


def _reordered_w_in(w_in):
    c0 = CONV_WIDTH
    xa, gb, gc, rest = w_in[..., :c0], w_in[..., c0:2 * c0], w_in[..., 2 * c0:3 * c0], w_in[..., 3 * c0:]
    return jnp.concatenate([xa, gc, rest[..., :POOL_WIDTH], gb, rest[..., POOL_WIDTH:]], axis=-1)


def kernel(x_prompt, x_sample, meta_tokens, norm1_g, w_in, conv_w, pool_w, pool_scale, mix_g,
           w_out, norm2_g, w_gate_up, w_down, final_g):
    depth = w_in.shape[0]
    pool_bd = jnp.zeros((depth, POOL_WIDTH, POOL_WIDTH), F32)
    for g in range(POOL_GROUPS):
        sl = slice(g * POOL_GROUP_DIM, (g + 1) * POOL_GROUP_DIM)
        pool_bd = pool_bd.at[:, sl, sl].set(pool_w[:, g])
    p = dict(
        norm1_g=norm1_g[:, None, :],
        w_in=_reordered_w_in(w_in).astype(BF16),
        conv_w=conv_w,
        pool_bd=pool_bd.astype(BF16),
        pool_scale=pool_scale[:, None, :],
        mix_g=mix_g[:, None, :],
        mix_g_f=mix_g[:, None, AP_WIDTH:],
        w_out=w_out.astype(BF16),
        norm2_g=norm2_g[:, None, :],
        w_gate_up=w_gate_up.astype(BF16),
        w_down=w_down.astype(BF16),
        final_g=final_g[None, :],
    )
    y_prompt = _trunk(x_prompt, meta_tokens, p, tile=656, in_tiles=(656, 656), last_tile=1024, n1=200, n2=82, groups=1,
                      rows_b=2800)
    y_sample = _trunk(x_sample, meta_tokens, p, tile=688, in_tiles=(1376, 1040), last_tile=1024, n1=16, n2=257)
    return (y_prompt, y_sample)
```

```python
import functools

import numpy as np
import jax
import jax.numpy as jnp
from jax import lax
from jax.experimental import pallas as pl
from jax.experimental.pallas import tpu as pltpu

D_MODEL = 1024
N_META = 16
EPS = 1e-6
CONV_WIDTH = 384
POOL_WIDTH = 384
POOL_GROUPS = 4
POOL_GROUP_DIM = POOL_WIDTH // POOL_GROUPS
POOL_WINDOWS = (2, 4, 8, 16)
FOURIER_WIDTH = 256
FOURIER_HEAD_DIM = 64
AP_WIDTH = CONV_WIDTH + POOL_WIDTH
IN_WIDTH = 3 * CONV_WIDTH + POOL_WIDTH + FOURIER_WIDTH
D_FF = 2816
FF_CHUNK = 256

HALO = 8
MXU_K = 256
MAX_RANK1_ROWS = 4
SUBLANES = 8
LANES = 128
BF16_ROWS = 16
VMEM_LIMIT_BYTES = 56 * 1024 * 1024

F32 = jnp.float32
BF16 = jnp.bfloat16


def _rms_scale(x):
    return lax.rsqrt(jnp.mean(x * x, axis=-1, keepdims=True) + EPS)


def _dot(a, b):
    return jnp.dot(a, b, preferred_element_type=F32)


def _first_layer_specs(x, tile):
    B, S, D = x.shape
    e = N_META
    assert e == BF16_ROWS and tile % e == 0 and pl.cdiv(S + e, tile) * tile - 2 * e <= S
    te = tile // e
    return [
        pl.BlockSpec((1, e, D), lambda b, i: (b, jnp.maximum(i * te - 1, 0), 0)),
        pl.BlockSpec((pl.Element(1), pl.Element(tile - 2 * e), pl.Element(D)),
                     lambda b, i: (b, SUBLANES * (i * (tile // SUBLANES)), 0)),
        pl.BlockSpec((1, e, D), lambda b, i: (b, jnp.minimum((i + 1) * te - 2, S // e - 1), 0)),
        pl.BlockSpec((e, D), lambda b, i: (0, 0)),
    ]


def _first_layer_tile(head_ref, mid_ref, tail_ref, meta_ref):
    head = jnp.where(pl.program_id(1) == 0, meta_ref[...], head_ref[0])
    return jnp.concatenate([head, mid_ref[0], tail_ref[0]], axis=0)


def _in_kernel(*refs, tile, seq_len, first):
    n_tile_refs = 4 if first else 1
    hp_ref, hn_ref, g1_ref, win_ref, cw_ref, pbd_ref, ps_ref, mg_ref, icnt_ref, mixap_ref, xf_ref = (
        refs[n_tile_refs:])
    hm = _first_layer_tile(*refs[:4]) if first else refs[0][0]
    n = tile + 2 * HALO
    main = slice(HALO, HALO + tile)
    i = pl.program_id(1)
    is_last = i == pl.num_programs(1) - 1
    valid_in_last = seq_len - (pl.cdiv(seq_len, tile) - 1) * tile
    hp = jnp.where(i == 0, 0.0, hp_ref[0])
    hn = jnp.where(is_last, 0.0, hn_ref[0])
    if valid_in_last < tile:
        hm = jnp.concatenate([hm[:valid_in_last], jnp.where(is_last, 0.0, hm[valid_in_last:])], axis=0)
    hx = jnp.concatenate([hp, hm, hn], axis=0)
    u = (hx * _rms_scale(hx) * g1_ref[...]).astype(BF16)
    z = _dot(u, win_ref[...])
    c0 = CONV_WIDTH
    xa_cols, gc_cols, xp_cols, gb_cols = (slice(k * c0, (k + 1) * c0) for k in range(4))
    xf_ref[0] = z[main, 4 * c0:]
    ga = z[:, gc_cols] * z[:, xa_cols]
    xp = z[:, xp_cols]

    def shift(x, s):
        return pltpu.roll(x, s % n, axis=0)

    cw = cw_ref[...]
    conv = shift(ga, 1)[main] * cw[0:1] + ga[main] * cw[1:2] + shift(ga, -1)[main] * cw[2:3]
    a = z[main, gb_cols] * conv

    rights = [w - 1 - w // 2 for w in POOL_WINDOWS]
    lane = lax.broadcasted_iota(jnp.int32, (1, LANES), 1)
    cols = []
    for c in range(POOL_WIDTH // LANES):
        x = xp[:, c * LANES:(c + 1) * LANES]
        t2 = x + shift(x, 1)
        t4 = t2 + shift(t2, 2)
        if c == 0:
            lo, hi = t2, shift(t4, -rights[1])
        else:
            t8 = t4 + shift(t4, 4)
            if c == 1:
                lo, hi = shift(t4, -rights[1]), shift(t8, -rights[2])
            else:
                t8d = shift(t8, 1)
                lo, hi = shift(t8, -rights[2]), t8d + shift(t8d, -8)
        boundary = (c + 1) * POOL_GROUP_DIM - c * LANES
        cols.append(jnp.where(lane < boundary, lo[main], hi[main]))
    wsum = jnp.concatenate(cols, axis=-1)
    pm = wsum * icnt_ref[0] - xp[main]
    p = _dot(pm.astype(BF16), pbd_ref[...]) * ps_ref[...]

    mg = mg_ref[...]
    an = a * _rms_scale(a) * mg[:, :CONV_WIDTH]
    pn = p * _rms_scale(p) * mg[:, CONV_WIDTH:AP_WIDTH]
    mixap_ref[0] = jnp.concatenate([an, pn], axis=-1).astype(BF16)


def _inverse_counts(seq_len, tile, n_tiles):
    lane = lax.broadcasted_iota(jnp.int32, (1, 1, POOL_WIDTH), 2)
    left = jnp.zeros_like(lane)
    right = jnp.zeros_like(lane)
    for g, w in enumerate(POOL_WINDOWS):
        in_group = (lane // POOL_GROUP_DIM) == g
        left = jnp.where(in_group, w // 2, left)
        right = jnp.where(in_group, w - 1 - w // 2, right)
    first_row = jnp.asarray([0, tile, (n_tiles - 1) * tile], jnp.int32)[:, None, None]
    pos = first_row + lax.broadcasted_iota(jnp.int32, (1, tile, 1), 1)
    cnt = jnp.minimum(pos + right, seq_len - 1) - jnp.maximum(pos - left, 0) + 1
    return 1.0 / jnp.maximum(cnt, 1).astype(F32)


def _in_call(h, layer, p, inv_cnt, *, tile, meta=None):
    first = meta is not None
    B, rows, D = h.shape
    L = rows + N_META if first else rows
    nt = pl.cdiv(L, tile)
    assert nt >= 3 and (nt - 1) * tile + HALO <= L and tile >= HALO
    tb = tile // SUBLANES
    last8 = rows // SUBLANES - 1
    shift8 = (L - rows) // SUBLANES
    const = lambda b, i: (layer, 0, 0)
    tile_kind = lambda b, i: (jnp.where(i == 0, 0, jnp.where(i == nt - 1, 2, 1)), 0, 0)
    if first:
        tile_specs, tile_args = _first_layer_specs(h, tile), (h, h, h, meta)
    else:
        tile_specs, tile_args = [pl.BlockSpec((1, tile, D), lambda b, i: (b, i, 0))], (h,)
    return pl.pallas_call(
        functools.partial(_in_kernel, tile=tile, seq_len=L, first=first),
        grid=(B, nt),
        in_specs=tile_specs + [
            pl.BlockSpec((1, HALO, D), lambda b, i: (b, jnp.maximum(i * tb - 1 - shift8, 0), 0)),
            pl.BlockSpec((1, HALO, D), lambda b, i: (b, jnp.minimum((i + 1) * tb - shift8, last8), 0)),
            pl.BlockSpec((None, 1, D), const),
            pl.BlockSpec((None, D, IN_WIDTH), const),
            pl.BlockSpec((None, 3, CONV_WIDTH), const),
            pl.BlockSpec((None, POOL_WIDTH, POOL_WIDTH), const),
            pl.BlockSpec((None, 1, POOL_WIDTH), const),
            pl.BlockSpec((None, 1, D), const),
            pl.BlockSpec((1, tile, POOL_WIDTH), tile_kind),
        ],
        out_specs=[
            pl.BlockSpec((1, tile, AP_WIDTH), lambda b, i: (b, i, 0)),
            pl.BlockSpec((1, tile, FOURIER_WIDTH), lambda b, i: (b, i, 0)),
        ],
        out_shape=[
            jax.ShapeDtypeStruct((B, L, AP_WIDTH), BF16),
            jax.ShapeDtypeStruct((B, L, FOURIER_WIDTH), F32),
        ],
        compiler_params=pltpu.CompilerParams(
            dimension_semantics=("parallel", "parallel"), vmem_limit_bytes=VMEM_LIMIT_BYTES),
        name="in_proj_mixers",
    )(*tile_args, h, h, p["norm1_g"], p["w_in"], p["conv_w"], p["pool_bd"], p["pool_scale"], p["mix_g"],
      inv_cnt)


def _dft_a_kernel(x_ref, ma_ref, o_ref, xs_ref, hr_ref, hi_ref, *, n2, n2p):
    per_step = xs_ref.shape[0]
    xs_ref[...] = jnp.swapaxes(x_ref[0], 0, 1)
    for j in range(per_step):
        hj = _dot(ma_ref[j], xs_ref[j].astype(BF16))
        hr_ref[j] = hj[:n2]
        hi_ref[j] = hj[n2p:n2p + n2]
    o_ref[0, 0] = jnp.swapaxes(hr_ref[...], 0, 1)
    o_ref[0, 1] = jnp.swapaxes(hi_ref[...], 0, 1)


def _dft_b_kernel(h_ref, wb_ref, c64_ref, s64_ref, o_ref, *, rows, chunk, scale):
    c64, s64 = c64_ref[...], s64_ref[...]
    for r0 in range(0, rows, chunk):
        m = min(chunk, rows - r0)
        wc, ws = wb_ref[0, :m, :m], wb_ref[1, :m, :m]
        hr = h_ref[0, 0, r0:r0 + m, :].astype(BF16)
        hi = h_ref[0, 1, r0:r0 + m, :].astype(BF16)
        yr = _dot(wc, hr) + _dot(ws, hi)
        yi = _dot(wc, hi) - _dot(ws, hr)
        f = _dot(yr.astype(BF16), c64) + _dot(yi.astype(BF16), s64)
        o_ref[0, r0:r0 + m, :] = f * scale


def _unpermute_kernel(x_ref, o_ref, *, n2):
    for j in range(SUBLANES):
        o_ref[0, j * n2:(j + 1) * n2, :] = x_ref[0, :, j, :]


def _dft_rows_kernel(x_ref, ma_ref, wb_ref, c64_ref, s64_ref, o_ref, xs_ref, h_ref, *, n1, n2, n2p, scale):
    k_main = n2 - n2 % MXU_K if n2 % MXU_K <= MAX_RANK1_ROWS else n2
    xs_ref[...] = jnp.swapaxes(x_ref[0], 0, 1)
    for j in range(n1):
        hj = _dot(ma_ref[j, :, :k_main], xs_ref[j, :k_main, :].astype(BF16))
        for r in range(k_main, n2):
            hj += ma_ref[j, :, r:r + 1].astype(F32) * xs_ref[j, r:r + 1, :].astype(BF16).astype(F32)
        h_ref[j] = hj.astype(BF16)
    c64, s64 = c64_ref[...], s64_ref[...]
    wc, ws = wb_ref[0], wb_ref[1]
    run = BF16_ROWS
    for k0 in range(0, n2, run):
        keep = min(run, n2 - k0)
        hr = jnp.concatenate([h_ref[l1, k0:k0 + run, :] for l1 in range(n1)], axis=0)
        hi = jnp.concatenate([h_ref[l1, n2p + k0:n2p + k0 + run, :] for l1 in range(n1)], axis=0)
        yr = _dot(wc, hr) + _dot(ws, hi)
        yi = _dot(wc, hi) - _dot(ws, hr)
        f = (_dot(yr.astype(BF16), c64) + _dot(yi.astype(BF16), s64)) * scale
        for k1 in range(n1):
            o_ref[0, n2 * k1 + k0:n2 * k1 + k0 + keep, :] = f[run * k1:run * k1 + keep]


def _angle(idx, period):
    return (idx % period).astype(F32) * (2.0 * np.pi / period)


def _dft_tables(n1, n2, groups, l1_major):
    n = n1 * n2
    n2p = -(-n2 // BF16_ROWS) * BF16_ROWS
    l1, k2 = (lax.broadcasted_iota(jnp.int32, (n1, n2p, 1), d) for d in range(2))
    k2b, l2 = (lax.broadcasted_iota(jnp.int32, (1, n2p, n2), d) for d in (1, 2))
    a, b = _angle(k2 * l1, n), _angle(k2b * l2, n2)
    ca, sa, cb, sb = jnp.cos(a), jnp.sin(a), jnp.cos(b), jnp.sin(b)
    ma = jnp.concatenate([ca * cb - sa * sb, -(sa * cb + ca * sb)], axis=1).astype(BF16)
    g = groups * n1
    r, c = (lax.broadcasted_iota(jnp.int32, (g, g), d) for d in range(2))
    if l1_major:
        angb = _angle((r // groups) * (c // groups), n1)
        same = (r % groups) == (c % groups)
    else:
        angb = _angle((r % n1) * (c % n1), n1)
        same = (r // n1) == (c // n1)
    wb = jnp.stack([jnp.where(same, jnp.cos(angb), 0.0), jnp.where(same, jnp.sin(angb), 0.0)]).astype(BF16)
    d, e = (lax.broadcasted_iota(jnp.int32, (FOURIER_WIDTH, FOURIER_WIDTH), d) for d in range(2))
    same_head = (d // FOURIER_HEAD_DIM) == (e // FOURIER_HEAD_DIM)
    ang64 = _angle(d * e, FOURIER_HEAD_DIM)
    return dict(ma=ma, wb=wb,
                c64=jnp.where(same_head, jnp.cos(ang64), 0.0).astype(BF16),
                s64=jnp.where(same_head, jnp.sin(ang64), 0.0).astype(BF16))


def _dft_rows_call(xf, tabs, *, n1, n2):
    B, L, W = xf.shape
    n2p = tabs["ma"].shape[1] // 2
    g = BF16_ROWS * n1
    return pl.pallas_call(
        functools.partial(_dft_rows_kernel, n1=n1, n2=n2, n2p=n2p,
                          scale=float(1.0 / np.sqrt(L * FOURIER_HEAD_DIM))),
        grid=(B,),
        in_specs=[
            pl.BlockSpec((1, n2, n1, W), lambda b: (b, 0, 0, 0)),
            pl.BlockSpec((n1, 2 * n2p, n2), lambda b: (0, 0, 0)),
            pl.BlockSpec((2, g, g), lambda b: (0, 0, 0)),
            pl.BlockSpec((W, W), lambda b: (0, 0)),
            pl.BlockSpec((W, W), lambda b: (0, 0)),
        ],
        out_specs=pl.BlockSpec((1, L, W), lambda b: (b, 0, 0)),
        out_shape=jax.ShapeDtypeStruct((B, L, W), F32),
        scratch_shapes=[pltpu.VMEM((n1, n2, W), F32), pltpu.VMEM((n1, 2 * n2p, W), BF16)],
        compiler_params=pltpu.CompilerParams(
            dimension_semantics=("parallel",), vmem_limit_bytes=VMEM_LIMIT_BYTES),
        name="seq_dft_rows",
    )(xf.reshape(B, n2, n1, W), tabs["ma"], tabs["wb"], tabs["c64"], tabs["s64"])


def _dft_call(xf, tabs, *, n1, n2, groups, rows_b, natural):
    B, L, W = xf.shape
    n2p = tabs["ma"].shape[1] // 2
    chunk = groups * n1
    params = pltpu.CompilerParams(
        dimension_semantics=("parallel", "parallel"), vmem_limit_bytes=VMEM_LIMIT_BYTES)
    nblk = n1 // SUBLANES
    a_rows = max(r for r in range(SUBLANES, 6 * SUBLANES, SUBLANES) if n1 % r == 0)
    h = pl.pallas_call(
        functools.partial(_dft_a_kernel, n2=n2, n2p=n2p),
        grid=(B, pl.cdiv(n1, a_rows)),
        in_specs=[
            pl.BlockSpec((1, n2, a_rows, W), lambda b, i: (b, 0, i, 0)),
            pl.BlockSpec((a_rows, 2 * n2p, n2), lambda b, i: (i, 0, 0)),
        ],
        out_specs=pl.BlockSpec((1, 2, n2, a_rows, W), lambda b, i: (b, 0, 0, i, 0)),
        out_shape=jax.ShapeDtypeStruct((B, 2, n2, n1, W), F32),
        scratch_shapes=[pltpu.VMEM((a_rows, n2, W), F32)] * 3,
        compiler_params=params,
        name="seq_dft_stage_a",
    )(xf.reshape(B, n2, n1, W), tabs["ma"])
    const2 = lambda b, i: (0, 0)
    f = pl.pallas_call(
        functools.partial(_dft_b_kernel, rows=rows_b, chunk=chunk,
                          scale=float(1.0 / np.sqrt(L * FOURIER_HEAD_DIM))),
        grid=(B, pl.cdiv(L, rows_b)),
        in_specs=[
            pl.BlockSpec((1, 2, rows_b, W), lambda b, i: (b, 0, i, 0)),
            pl.BlockSpec((2, chunk, chunk), lambda b, i: (0, 0, 0)),
            pl.BlockSpec((W, W), const2),
            pl.BlockSpec((W, W), const2),
        ],
        out_specs=pl.BlockSpec((1, rows_b, W), lambda b, i: (b, i, 0)),
        out_shape=jax.ShapeDtypeStruct((B, L, W), F32),
        compiler_params=params,
        name="seq_dft_stage_b",
    )(h.reshape(B, 2, L, W), tabs["wb"], tabs["c64"], tabs["s64"])
    f = f.reshape(B, n2, n1, W)
    if not natural:
        return f
    return pl.pallas_call(
        functools.partial(_unpermute_kernel, n2=n2),
        grid=(B, nblk),
        in_specs=[pl.BlockSpec((1, n2, SUBLANES, W), lambda b, i: (b, 0, i, 0))],
        out_specs=pl.BlockSpec((1, SUBLANES * n2, W), lambda b, i: (b, i, 0)),
        out_shape=jax.ShapeDtypeStruct((B, L, W), F32),
        compiler_params=params,
        name="seq_dft_unpermute",
    )(f)


def _out_kernel(*refs, last, first, permuted_n2):
    n_tile_refs = 4 if first else 1
    mixap_ref, f_ref, mgf_ref, wout_ref, g2_ref, wgu_ref, wdn_ref, fg_ref, o_ref, act_ref, *scratch = (
        refs[n_tile_refs:])
    h = _first_layer_tile(*refs[:4]) if first else refs[0][0]
    if permuted_n2 is None:
        f = f_ref[0]
    else:
        fbuf_ref, = scratch
        fj = jnp.swapaxes(f_ref[0], 0, 1)
        for j in range(SUBLANES):
            fbuf_ref[j * permuted_n2:(j + 1) * permuted_n2, :] = fj[j]
        f = fbuf_ref[...]
    fn = (f * _rms_scale(f) * mgf_ref[...]).astype(BF16)
    mix = jnp.concatenate([mixap_ref[0], fn], axis=-1)
    h1 = h + _dot(mix, wout_ref[...])
    v = (h1 * _rms_scale(h1) * g2_ref[...]).astype(BF16)
    for c in range(D_FF // FF_CHUNK):
        lo = c * FF_CHUNK
        gate = _dot(v, wgu_ref[:, lo:lo + FF_CHUNK])
        up = _dot(v, wgu_ref[:, D_FF + lo:D_FF + lo + FF_CHUNK])
        act_ref[:, lo:lo + FF_CHUNK] = (gate * jax.nn.sigmoid(gate) * up).astype(BF16)
    h2 = h1 + _dot(act_ref[...], wdn_ref[...])
    if last:
        h2 = h2 * _rms_scale(h2) * fg_ref[...]
    o_ref[0] = h2


def _out_call(h, mixap, f, layer, p, *, tile, last, meta=None):
    first = meta is not None
    assert not (first and last)
    B, _, D = h.shape
    L = mixap.shape[1]
    row = lambda b, i: (b, i, 0)
    const = lambda b, i: (layer, 0, 0)
    single = pl.Buffered(1)
    scratch = [pltpu.VMEM((tile, D_FF), BF16)]
    permuted_n2 = None
    if last:
        out_rows = L - N_META
        assert out_rows % tile == 0 and f.ndim == 3 and N_META % BF16_ROWS == 0 and tile % BF16_ROWS == 0
        nt = out_rows // tile

        def window(width, align):
            return pl.BlockSpec((pl.Element(1), pl.Element(tile), pl.Element(width)),
                                lambda b, i: (b, align * (N_META // align + i * (tile // align)), 0))

        tile_specs, tile_args = [window(D, SUBLANES)], (h,)
        ap_spec, f_spec = window(AP_WIDTH, BF16_ROWS), window(FOURIER_WIDTH, SUBLANES)
    else:
        out_rows = L
        nt = pl.cdiv(L, tile)
        if first:
            tile_specs, tile_args = _first_layer_specs(h, tile), (h, h, h, meta)
        else:
            tile_specs, tile_args = [pl.BlockSpec((1, tile, D), row)], (h,)
        ap_spec = pl.BlockSpec((1, tile, AP_WIDTH), row)
        if f.ndim == 4:
            permuted_n2 = f.shape[1]
            assert tile == SUBLANES * permuted_n2 and f.shape[2] * permuted_n2 == L
            f_spec = pl.BlockSpec((1, permuted_n2, SUBLANES, FOURIER_WIDTH), lambda b, i: (b, 0, i, 0))
            scratch.append(pltpu.VMEM((tile, FOURIER_WIDTH), F32))
        else:
            f_spec = pl.BlockSpec((1, tile, FOURIER_WIDTH), row)
    return pl.pallas_call(
        functools.partial(_out_kernel, last=last, first=first, permuted_n2=permuted_n2),
        grid=(B, nt),
        in_specs=tile_specs + [
            ap_spec,
            f_spec,
            pl.BlockSpec((None, 1, FOURIER_WIDTH), const),
            pl.BlockSpec((None, D, D), const, pipeline_mode=single),
            pl.BlockSpec((None, 1, D), const),
            pl.BlockSpec((None, D, 2 * D_FF), const, pipeline_mode=single),
            pl.BlockSpec((None, D_FF, D), const, pipeline_mode=single),
            pl.BlockSpec((1, D), lambda b, i: (0, 0)),
        ],
        out_specs=pl.BlockSpec((1, tile, D), row),
        out_shape=jax.ShapeDtypeStruct((B, out_rows, D), F32),
        scratch_shapes=scratch,
        compiler_params=pltpu.CompilerParams(
            dimension_semantics=("parallel", "parallel"), vmem_limit_bytes=VMEM_LIMIT_BYTES),
        name="out_proj_ffn",
    )(*tile_args, mixap, f, p["mix_g_f"], p["w_out"], p["norm2_g"], p["w_gate_up"], p["w_down"],
      p["final_g"])


def _trunk(x, meta_tokens, p, *, tile, in_tiles, last_tile, n1, n2, groups=BF16_ROWS, rows_b=None):
    B, S, D = x.shape
    L = S + N_META
    assert n1 * n2 == L and n1 % SUBLANES == 0 and tile % BF16_ROWS == 0
    assert rows_b is None or rows_b % (groups * n1) in (0, L % (groups * n1))
    meta = meta_tokens.astype(x.dtype)
    h = x
    tabs = _dft_tables(n1, n2, groups, l1_major=rows_b is None)
    inv_cnt = {t: _inverse_counts(L, t, pl.cdiv(L, t)) for t in in_tiles}
    depth = p["w_in"].shape[0]
    for layer in range(depth):
        last = layer == depth - 1
        first = meta if layer == 0 else None
        t_in = in_tiles[min(layer, 1)]
        mixap, xf = _in_call(h, layer, p, inv_cnt[t_in], tile=t_in, meta=first)
        if rows_b is None:
            f = _dft_rows_call(xf, tabs, n1=n1, n2=n2)
        else:
            f = _dft_call(xf, tabs, n1=n1, n2=n2, groups=groups, rows_b=rows_b, natural=last)
        h = _out_call(h, mixap, f, layer, p, tile=last_tile if last else tile, last=last, meta=first)
    return h


def _reordered_w_in(w_in):
    c0 = CONV_WIDTH
    xa, gb, gc, rest = w_in[..., :c0], w_in[..., c0:2 * c0], w_in[..., 2 * c0:3 * c0], w_in[..., 3 * c0:]
    return jnp.concatenate([xa, gc, rest[..., :POOL_WIDTH], gb, rest[..., POOL_WIDTH:]], axis=-1)


def kernel(x_prompt, x_sample, meta_tokens, norm1_g, w_in, conv_w, pool_w, pool_scale, mix_g,
           w_out, norm2_g, w_gate_up, w_down, final_g):
    depth = w_in.shape[0]
    pool_bd = jnp.zeros((depth, POOL_WIDTH, POOL_WIDTH), F32)
    for g in range(POOL_GROUPS):
        sl = slice(g * POOL_GROUP_DIM, (g + 1) * POOL_GROUP_DIM)
        pool_bd = pool_bd.at[:, sl, sl].set(pool_w[:, g])
    p = dict(
        norm1_g=norm1_g[:, None, :],
        w_in=_reordered_w_in(w_in).astype(BF16),
        conv_w=conv_w,
        pool_bd=pool_bd.astype(BF16),
        pool_scale=pool_scale[:, None, :],
        mix_g=mix_g[:, None, :],
        mix_g_f=mix_g[:, None, AP_WIDTH:],
        w_out=w_out.astype(BF16),
        norm2_g=norm2_g[:, None, :],
        w_gate_up=w_gate_up.astype(BF16),
        w_down=w_down.astype(BF16),
        final_g=final_g[None, :],
    )
    y_prompt = _trunk(x_prompt, meta_tokens, p, tile=656, in_tiles=(656, 656), last_tile=1024, n1=200, n2=82, groups=1,
                      rows_b=2800)
    y_sample = _trunk(x_sample, meta_tokens, p, tile=688, in_tiles=(1376, 1040), last_tile=1024, n1=16, n2=257)
    return (y_prompt, y_sample)
```

```python
import functools

import numpy as np
import jax
import jax.numpy as jnp
from jax import lax
from jax.experimental import pallas as pl
from jax.experimental.pallas import tpu as pltpu

D_MODEL = 1024
N_META = 16
EPS = 1e-6
CONV_WIDTH = 384
POOL_WIDTH = 384
POOL_GROUPS = 4
POOL_GROUP_DIM = POOL_WIDTH // POOL_GROUPS
POOL_WINDOWS = (2, 4, 8, 16)
FOURIER_WIDTH = 256
FOURIER_HEAD_DIM = 64
AP_WIDTH = CONV_WIDTH + POOL_WIDTH
IN_WIDTH = 3 * CONV_WIDTH + POOL_WIDTH + FOURIER_WIDTH
D_FF = 2816
FF_CHUNK = 256

HALO = 8
MXU_K = 256
MAX_RANK1_ROWS = 4
SUBLANES = 8
LANES = 128
BF16_ROWS = 16
VMEM_LIMIT_BYTES = 56 * 1024 * 1024

F32 = jnp.float32
BF16 = jnp.bfloat16


def _rms_scale(x):
    return lax.rsqrt(jnp.mean(x * x, axis=-1, keepdims=True) + EPS)


def _dot(a, b):
    return jnp.dot(a, b, preferred_element_type=F32)


def _first_layer_specs(x, tile):
    B, S, D = x.shape
    e = N_META
    assert e == BF16_ROWS and tile % e == 0 and pl.cdiv(S + e, tile) * tile - 2 * e <= S
    te = tile // e
    return [
        pl.BlockSpec((1, e, D), lambda b, i: (b, jnp.maximum(i * te - 1, 0), 0)),
        pl.BlockSpec((pl.Element(1), pl.Element(tile - 2 * e), pl.Element(D)),
                     lambda b, i: (b, SUBLANES * (i * (tile // SUBLANES)), 0)),
        pl.BlockSpec((1, e, D), lambda b, i: (b, jnp.minimum((i + 1) * te - 2, S // e - 1), 0)),
        pl.BlockSpec((e, D), lambda b, i: (0, 0)),
    ]


def _first_layer_tile(head_ref, mid_ref, tail_ref, meta_ref):
    head = jnp.where(pl.program_id(1) == 0, meta_ref[...], head_ref[0])
    return jnp.concatenate([head, mid_ref[0], tail_ref[0]], axis=0)


def _in_kernel(*refs, tile, seq_len, first):
    n_tile_refs = 4 if first else 1
    hp_ref, hn_ref, g1_ref, win_ref, cw_ref, pbd_ref, ps_ref, mg_ref, icnt_ref, mixap_ref, xf_ref = (
        refs[n_tile_refs:])
    hm = _first_layer_tile(*refs[:4]) if first else refs[0][0]
    n = tile + 2 * HALO
    main = slice(HALO, HALO + tile)
    i = pl.program_id(1)
    is_last = i == pl.num_programs(1) - 1
    valid_in_last = seq_len - (pl.cdiv(seq_len, tile) - 1) * tile
    hp = jnp.where(i == 0, 0.0, hp_ref[0])
    hn = jnp.where(is_last, 0.0, hn_ref[0])
    if valid_in_last < tile:
        hm = jnp.concatenate([hm[:valid_in_last], jnp.where(is_last, 0.0, hm[valid_in_last:])], axis=0)
    hx = jnp.concatenate([hp, hm, hn], axis=0)
    u = (hx * _rms_scale(hx) * g1_ref[...]).astype(BF16)
    z = _dot(u, win_ref[...])
    c0 = CONV_WIDTH
    xa_cols, gc_cols, xp_cols, gb_cols = (slice(k * c0, (k + 1) * c0) for k in range(4))
    xf_ref[0] = z[main, 4 * c0:]
    ga = z[:, gc_cols] * z[:, xa_cols]
    xp = z[:, xp_cols]

    def shift(x, s):
        return pltpu.roll(x, s % n, axis=0)

    cw = cw_ref[...]
    conv = shift(ga, 1)[main] * cw[0:1] + ga[main] * cw[1:2] + shift(ga, -1)[main] * cw[2:3]
    a = z[main, gb_cols] * conv

    rights = [w - 1 - w // 2 for w in POOL_WINDOWS]
    lane = lax.broadcasted_iota(jnp.int32, (1, LANES), 1)
    cols = []
    for c in range(POOL_WIDTH // LANES):
        x = xp[:, c * LANES:(c + 1) * LANES]
        t2 = x + shift(x, 1)
        t4 = t2 + shift(t2, 2)
        if c == 0:
            lo, hi = t2, shift(t4, -rights[1])
        else:
            t8 = t4 + shift(t4, 4)
            if c == 1:
                lo, hi = shift(t4, -rights[1]), shift(t8, -rights[2])
            else:
                t8d = shift(t8, 1)
                lo, hi = shift(t8, -rights[2]), t8d + shift(t8d, -8)
        boundary = (c + 1) * POOL_GROUP_DIM - c * LANES
        cols.append(jnp.where(lane < boundary, lo[main], hi[main]))
    wsum = jnp.concatenate(cols, axis=-1)
    pm = wsum * icnt_ref[0] - xp[main]
    p = _dot(pm.astype(BF16), pbd_ref[...]) * ps_ref[...]

    mg = mg_ref[...]
    an = a * _rms_scale(a) * mg[:, :CONV_WIDTH]
    pn = p * _rms_scale(p) * mg[:, CONV_WIDTH:AP_WIDTH]
    mixap_ref[0] = jnp.concatenate([an, pn], axis=-1).astype(BF16)


def _inverse_counts(seq_len, tile, n_tiles):
    lane = lax.broadcasted_iota(jnp.int32, (1, 1, POOL_WIDTH), 2)
    left = jnp.zeros_like(lane)
    right = jnp.zeros_like(lane)
    for g, w in enumerate(POOL_WINDOWS):
        in_group = (lane // POOL_GROUP_DIM) == g
        left = jnp.where(in_group, w // 2, left)
        right = jnp.where(in_group, w - 1 - w // 2, right)
    first_row = jnp.asarray([0, tile, (n_tiles - 1) * tile], jnp.int32)[:, None, None]
    pos = first_row + lax.broadcasted_iota(jnp.int32, (1, tile, 1), 1)
    cnt = jnp.minimum(pos + right, seq_len - 1) - jnp.maximum(pos - left, 0) + 1
    return 1.0 / jnp.maximum(cnt, 1).astype(F32)


def _in_call(h, layer, p, inv_cnt, *, tile, meta=None):
    first = meta is not None
    B, rows, D = h.shape
    L = rows + N_META if first else rows
    nt = pl.cdiv(L, tile)
    assert nt >= 3 and (nt - 1) * tile + HALO <= L and tile >= HALO
    tb = tile // SUBLANES
    last8 = rows // SUBLANES - 1
    shift8 = (L - rows) // SUBLANES
    const = lambda b, i: (layer, 0, 0)
    tile_kind = lambda b, i: (jnp.where(i == 0, 0, jnp.where(i == nt - 1, 2, 1)), 0, 0)
    if first:
        tile_specs, tile_args = _first_layer_specs(h, tile), (h, h, h, meta)
    else:
        tile_specs, tile_args = [pl.BlockSpec((1, tile, D), lambda b, i: (b, i, 0))], (h,)
    return pl.pallas_call(
        functools.partial(_in_kernel, tile=tile, seq_len=L, first=first),
        grid=(B, nt),
        in_specs=tile_specs + [
            pl.BlockSpec((1, HALO, D), lambda b, i: (b, jnp.maximum(i * tb - 1 - shift8, 0), 0)),
            pl.BlockSpec((1, HALO, D), lambda b, i: (b, jnp.minimum((i + 1) * tb - shift8, last8), 0)),
            pl.BlockSpec((None, 1, D), const),
            pl.BlockSpec((None, D, IN_WIDTH), const),
            pl.BlockSpec((None, 3, CONV_WIDTH), const),
            pl.BlockSpec((None, POOL_WIDTH, POOL_WIDTH), const),
            pl.BlockSpec((None, 1, POOL_WIDTH), const),
            pl.BlockSpec((None, 1, D), const),
            pl.BlockSpec((1, tile, POOL_WIDTH), tile_kind),
        ],
        out_specs=[
            pl.BlockSpec((1, tile, AP_WIDTH), lambda b, i: (b, i, 0)),
            pl.BlockSpec((1, tile, FOURIER_WIDTH), lambda b, i: (b, i, 0)),
        ],
        out_shape=[
            jax.ShapeDtypeStruct((B, L, AP_WIDTH), BF16),
            jax.ShapeDtypeStruct((B, L, FOURIER_WIDTH), F32),
        ],
        compiler_params=pltpu.CompilerParams(
            dimension_semantics=("parallel", "parallel"), vmem_limit_bytes=VMEM_LIMIT_BYTES),
        name="in_proj_mixers",
    )(*tile_args, h, h, p["norm1_g"], p["w_in"], p["conv_w"], p["pool_bd"], p["pool_scale"], p["mix_g"],
      inv_cnt)


def _dft_a_kernel(x_ref, ma_ref, o_ref, xs_ref, hr_ref, hi_ref, *, n2, n2p):
    per_step = xs_ref.shape[0]
    xs_ref[...] = jnp.swapaxes(x_ref[0], 0, 1)
    for j in range(per_step):
        hj = _dot(ma_ref[j], xs_ref[j].astype(BF16)).astype(BF16)
        hr_ref[j] = hj[:n2]
        hi_ref[j] = hj[n2p:n2p + n2]
    o_ref[0, 0] = jnp.swapaxes(hr_ref[...], 0, 1)
    o_ref[0, 1] = jnp.swapaxes(hi_ref[...], 0, 1)


def _dft_b_kernel(h_ref, wb_ref, c64_ref, s64_ref, o_ref, *, scale):
    c64, s64 = c64_ref[...], s64_ref[...]
    wc, ws = wb_ref[0], wb_ref[1]
    for q in range(h_ref.shape[2]):
        hr, hi = h_ref[0, 0, q], h_ref[0, 1, q]
        yr = _dot(wc, hr) + _dot(ws, hi)
        yi = _dot(wc, hi) - _dot(ws, hr)
        f = _dot(yr.astype(BF16), c64) + _dot(yi.astype(BF16), s64)
        o_ref[0, q] = f * scale


def _unpermute_kernel(x_ref, o_ref, *, n2):
    for j in range(SUBLANES):
        o_ref[0, j * n2:(j + 1) * n2, :] = x_ref[0, :, j, :]


def _dft_rows_kernel(x_ref, ma_ref, wb_ref, c64_ref, s64_ref, o_ref, xs_ref, h_ref, *, n1, n2, n2p, scale):
    k_main = n2 - n2 % MXU_K if n2 % MXU_K <= MAX_RANK1_ROWS else n2
    xs_ref[...] = jnp.swapaxes(x_ref[0], 0, 1)
    for j in range(n1):
        hj = _dot(ma_ref[j, :, :k_main], xs_ref[j, :k_main, :].astype(BF16))
        for r in range(k_main, n2):
            hj += ma_ref[j, :, r:r + 1].astype(F32) * xs_ref[j, r:r + 1, :].astype(BF16).astype(F32)
        h_ref[j] = hj.astype(BF16)
    c64, s64 = c64_ref[...], s64_ref[...]
    wc, ws = wb_ref[0], wb_ref[1]
    run = BF16_ROWS
    for k0 in range(0, n2, run):
        keep = min(run, n2 - k0)
        hr = jnp.concatenate([h_ref[l1, k0:k0 + run, :] for l1 in range(n1)], axis=0)
        hi = jnp.concatenate([h_ref[l1, n2p + k0:n2p + k0 + run, :] for l1 in range(n1)], axis=0)
        yr = _dot(wc, hr) + _dot(ws, hi)
        yi = _dot(wc, hi) - _dot(ws, hr)
        f = (_dot(yr.astype(BF16), c64) + _dot(yi.astype(BF16), s64)) * scale
        for k1 in range(n1):
            o_ref[0, n2 * k1 + k0:n2 * k1 + k0 + keep, :] = f[run * k1:run * k1 + keep]


def _angle(idx, period):
    return (idx % period).astype(F32) * (2.0 * np.pi / period)


def _dft_tables(n1, n2, groups, l1_major):
    n = n1 * n2
    n2p = -(-n2 // BF16_ROWS) * BF16_ROWS
    l1, k2 = (lax.broadcasted_iota(jnp.int32, (n1, n2p, 1), d) for d in range(2))
    k2b, l2 = (lax.broadcasted_iota(jnp.int32, (1, n2p, n2), d) for d in (1, 2))
    a, b = _angle(k2 * l1, n), _angle(k2b * l2, n2)
    ca, sa, cb, sb = jnp.cos(a), jnp.sin(a), jnp.cos(b), jnp.sin(b)
    ma = jnp.concatenate([ca * cb - sa * sb, -(sa * cb + ca * sb)], axis=1).astype(BF16)
    g = groups * n1
    r, c = (lax.broadcasted_iota(jnp.int32, (g, g), d) for d in range(2))
    if l1_major:
        angb = _angle((r // groups) * (c // groups), n1)
        same = (r % groups) == (c % groups)
    else:
        angb = _angle((r % n1) * (c % n1), n1)
        same = (r // n1) == (c // n1)
    wb = jnp.stack([jnp.where(same, jnp.cos(angb), 0.0), jnp.where(same, jnp.sin(angb), 0.0)]).astype(BF16)
    d, e = (lax.broadcasted_iota(jnp.int32, (FOURIER_WIDTH, FOURIER_WIDTH), d) for d in range(2))
    same_head = (d // FOURIER_HEAD_DIM) == (e // FOURIER_HEAD_DIM)
    ang64 = _angle(d * e, FOURIER_HEAD_DIM)
    return dict(ma=ma, wb=wb,
                c64=jnp.where(same_head, jnp.cos(ang64), 0.0).astype(BF16),
                s64=jnp.where(same_head, jnp.sin(ang64), 0.0).astype(BF16))


def _dft_rows_call(xf, tabs, *, n1, n2):
    B, L, W = xf.shape
    n2p = tabs["ma"].shape[1] // 2
    g = BF16_ROWS * n1
    return pl.pallas_call(
        functools.partial(_dft_rows_kernel, n1=n1, n2=n2, n2p=n2p,
                          scale=float(1.0 / np.sqrt(L * FOURIER_HEAD_DIM))),
        grid=(B,),
        in_specs=[
            pl.BlockSpec((1, n2, n1, W), lambda b: (b, 0, 0, 0)),
            pl.BlockSpec((n1, 2 * n2p, n2), lambda b: (0, 0, 0)),
            pl.BlockSpec((2, g, g), lambda b: (0, 0, 0)),
            pl.BlockSpec((W, W), lambda b: (0, 0)),
            pl.BlockSpec((W, W), lambda b: (0, 0)),
        ],
        out_specs=pl.BlockSpec((1, L, W), lambda b: (b, 0, 0)),
        out_shape=jax.ShapeDtypeStruct((B, L, W), F32),
        scratch_shapes=[pltpu.VMEM((n1, n2, W), F32), pltpu.VMEM((n1, 2 * n2p, W), BF16)],
        compiler_params=pltpu.CompilerParams(
            dimension_semantics=("parallel",), vmem_limit_bytes=VMEM_LIMIT_BYTES),
        name="seq_dft_rows",
    )(xf.reshape(B, n2, n1, W), tabs["ma"], tabs["wb"], tabs["c64"], tabs["s64"])


def _dft_call(xf, tabs, *, n1, n2, a_rows, k2_per_step, natural):
    B, L, W = xf.shape
    n2p = tabs["ma"].shape[1] // 2
    assert a_rows % BF16_ROWS == 0 and tabs["wb"].shape[1] == n1
    params = pltpu.CompilerParams(
        dimension_semantics=("parallel", "parallel"), vmem_limit_bytes=VMEM_LIMIT_BYTES)
    h = pl.pallas_call(
        functools.partial(_dft_a_kernel, n2=n2, n2p=n2p),
        grid=(B, pl.cdiv(n1, a_rows)),
        in_specs=[
            pl.BlockSpec((1, n2, a_rows, W), lambda b, i: (b, 0, i, 0)),
            pl.BlockSpec((a_rows, 2 * n2p, n2), lambda b, i: (i, 0, 0)),
        ],
        out_specs=pl.BlockSpec((1, 2, n2, a_rows, W), lambda b, i: (b, 0, 0, i, 0)),
        out_shape=jax.ShapeDtypeStruct((B, 2, n2, n1, W), BF16),
        scratch_shapes=[pltpu.VMEM((a_rows, n2, W), F32)] + [pltpu.VMEM((a_rows, n2, W), BF16)] * 2,
        compiler_params=params,
        name="seq_dft_stage_a",
    )(xf.reshape(B, n2, n1, W), tabs["ma"])
    const2 = lambda b, i: (0, 0)
    f = pl.pallas_call(
        functools.partial(_dft_b_kernel, scale=float(1.0 / np.sqrt(L * FOURIER_HEAD_DIM))),
        grid=(B, pl.cdiv(n2, k2_per_step)),
        in_specs=[
            pl.BlockSpec((1, 2, k2_per_step, n1, W), lambda b, i: (b, 0, i, 0, 0)),
            pl.BlockSpec((2, n1, n1), lambda b, i: (0, 0, 0)),
            pl.BlockSpec((W, W), const2),
            pl.BlockSpec((W, W), const2),
        ],
        out_specs=pl.BlockSpec((1, k2_per_step, n1, W), lambda b, i: (b, i, 0, 0)),
        out_shape=jax.ShapeDtypeStruct((B, n2, n1, W), F32),
        compiler_params=params,
        name="seq_dft_stage_b",
    )(h, tabs["wb"], tabs["c64"], tabs["s64"])
    if not natural:
        return f
    return pl.pallas_call(
        functools.partial(_unpermute_kernel, n2=n2),
        grid=(B, n1 // SUBLANES),
        in_specs=[pl.BlockSpec((1, n2, SUBLANES, W), lambda b, i: (b, 0, i, 0))],
        out_specs=pl.BlockSpec((1, SUBLANES * n2, W), lambda b, i: (b, i, 0)),
        out_shape=jax.ShapeDtypeStruct((B, L, W), F32),
        compiler_params=params,
        name="seq_dft_unpermute",
    )(f)


def _out_kernel(*refs, last, first, permuted_n2):
    n_tile_refs = 4 if first else 1
    mixap_ref, f_ref, mgf_ref, wout_ref, g2_ref, wgu_ref, wdn_ref, fg_ref, o_ref, act_ref, *scratch = (
        refs[n_tile_refs:])
    h = _first_layer_tile(*refs[:4]) if first else refs[0][0]
    if permuted_n2 is None:
        f = f_ref[0]
    else:
        fbuf_ref, = scratch
        fj = jnp.swapaxes(f_ref[0], 0, 1)
        for j in range(SUBLANES):
            fbuf_ref[j * permuted_n2:(j + 1) * permuted_n2, :] = fj[j]
        f = fbuf_ref[...]
    fn = (f * _rms_scale(f) * mgf_ref[...]).astype(BF16)
    mix = jnp.concatenate([mixap_ref[0], fn], axis=-1)
    h1 = h + _dot(mix, wout_ref[...])
    v = (h1 * _rms_scale(h1) * g2_ref[...]).astype(BF16)
    for c in range(D_FF // FF_CHUNK):
        lo = c * FF_CHUNK
        gate = _dot(v, wgu_ref[:, lo:lo + FF_CHUNK])
        up = _dot(v, wgu_ref[:, D_FF + lo:D_FF + lo + FF_CHUNK])
        act_ref[:, lo:lo + FF_CHUNK] = (gate * jax.nn.sigmoid(gate) * up).astype(BF16)
    h2 = h1 + _dot(act_ref[...], wdn_ref[...])
    if last:
        h2 = h2 * _rms_scale(h2) * fg_ref[...]
    o_ref[0] = h2


def _out_call(h, mixap, f, layer, p, *, tile, last, meta=None):
    first = meta is not None
    assert not (first and last)
    B, _, D = h.shape
    L = mixap.shape[1]
    row = lambda b, i: (b, i, 0)
    const = lambda b, i: (layer, 0, 0)
    single = pl.Buffered(1)
    scratch = [pltpu.VMEM((tile, D_FF), BF16)]
    permuted_n2 = None
    if last:
        out_rows = L - N_META
        assert out_rows % tile == 0 and f.ndim == 3 and N_META % BF16_ROWS == 0 and tile % BF16_ROWS == 0
        nt = out_rows // tile

        def window(width, align):
            return pl.BlockSpec((pl.Element(1), pl.Element(tile), pl.Element(width)),
                                lambda b, i: (b, align * (N_META // align + i * (tile // align)), 0))

        tile_specs, tile_args = [window(D, SUBLANES)], (h,)
        ap_spec, f_spec = window(AP_WIDTH, BF16_ROWS), window(FOURIER_WIDTH, SUBLANES)
    else:
        out_rows = L
        nt = pl.cdiv(L, tile)
        if first:
            tile_specs, tile_args = _first_layer_specs(h, tile), (h, h, h, meta)
        else:
            tile_specs, tile_args = [pl.BlockSpec((1, tile, D), row)], (h,)
        ap_spec = pl.BlockSpec((1, tile, AP_WIDTH), row)
        if f.ndim == 4:
            permuted_n2 = f.shape[1]
            assert tile == SUBLANES * permuted_n2 and f.shape[2] * permuted_n2 == L
            f_spec = pl.BlockSpec((1, permuted_n2, SUBLANES, FOURIER_WIDTH), lambda b, i: (b, 0, i, 0))
            scratch.append(pltpu.VMEM((tile, FOURIER_WIDTH), F32))
        else:
            f_spec = pl.BlockSpec((1, tile, FOURIER_WIDTH), row)
    return pl.pallas_call(
        functools.partial(_out_kernel, last=last, first=first, permuted_n2=permuted_n2),
        grid=(B, nt),
        in_specs=tile_specs + [
            ap_spec,
            f_spec,
            pl.BlockSpec((None, 1, FOURIER_WIDTH), const),
            pl.BlockSpec((None, D, D), const, pipeline_mode=single),
            pl.BlockSpec((None, 1, D), const),
            pl.BlockSpec((None, D, 2 * D_FF), const, pipeline_mode=single),
            pl.BlockSpec((None, D_FF, D), const, pipeline_mode=single),
            pl.BlockSpec((1, D), lambda b, i: (0, 0)),
        ],
        out_specs=pl.BlockSpec((1, tile, D), row),
        out_shape=jax.ShapeDtypeStruct((B, out_rows, D), F32),
        scratch_shapes=scratch,
        compiler_params=pltpu.CompilerParams(
            dimension_semantics=("parallel", "parallel"), vmem_limit_bytes=VMEM_LIMIT_BYTES),
        name="out_proj_ffn",
    )(*tile_args, mixap, f, p["mix_g_f"], p["w_out"], p["norm2_g"], p["w_gate_up"], p["w_down"],
      p["final_g"])


def _trunk(x, meta_tokens, p, *, tile, in_tiles, last_tile, n1, n2, dft_steps=None):
    B, S, D = x.shape
    L = S + N_META
    assert n1 * n2 == L and n1 % SUBLANES == 0 and tile % BF16_ROWS == 0
    meta = meta_tokens.astype(x.dtype)
    h = x
    tabs = _dft_tables(n1, n2, BF16_ROWS if dft_steps is None else 1, l1_major=dft_steps is None)
    inv_cnt = {t: _inverse_counts(L, t, pl.cdiv(L, t)) for t in in_tiles}
    depth = p["w_in"].shape[0]
    for layer in range(depth):
        last = layer == depth - 1
        first = meta if layer == 0 else None
        t_in = in_tiles[min(layer, 1)]
        mixap, xf = _in_call(h, layer, p, inv_cnt[t_in], tile=t_in, meta=first)
        if dft_steps is None:
            f = _dft_rows_call(xf, tabs, n1=n1, n2=n2)
        else:
            f = _dft_call(xf, tabs, n1=n1, n2=n2, a_rows=dft_steps[0], k2_per_step=dft_steps[1], natural=last)
        h = _out_call(h, mixap, f, layer, p, tile=last_tile if last else tile, last=last, meta=first)
    return h


def _reordered_w_in(w_in):
    c0 = CONV_WIDTH
    xa, gb, gc, rest = w_in[..., :c0], w_in[..., c0:2 * c0], w_in[..., 2 * c0:3 * c0], w_in[..., 3 * c0:]
    return jnp.concatenate([xa, gc, rest[..., :POOL_WIDTH], gb, rest[..., POOL_WIDTH:]], axis=-1)


def kernel(x_prompt, x_sample, meta_tokens, norm1_g, w_in, conv_w, pool_w, pool_scale, mix_g,
           w_out, norm2_g, w_gate_up, w_down, final_g):
    depth = w_in.shape[0]
    pool_bd = jnp.zeros((depth, POOL_WIDTH, POOL_WIDTH), F32)
    for g in range(POOL_GROUPS):
        sl = slice(g * POOL_GROUP_DIM, (g + 1) * POOL_GROUP_DIM)
        pool_bd = pool_bd.at[:, sl, sl].set(pool_w[:, g])
    p = dict(
        norm1_g=norm1_g[:, None, :],
        w_in=_reordered_w_in(w_in).astype(BF16),
        conv_w=conv_w,
        pool_bd=pool_bd.astype(BF16),
        pool_scale=pool_scale[:, None, :],
        mix_g=mix_g[:, None, :],
        mix_g_f=mix_g[:, None, AP_WIDTH:],
        w_out=w_out.astype(BF16),
        norm2_g=norm2_g[:, None, :],
        w_gate_up=w_gate_up.astype(BF16),
        w_down=w_down.astype(BF16),
        final_g=final_g[None, :],
    )
    y_prompt = _trunk(x_prompt, meta_tokens, p, tile=656, in_tiles=(656, 656), last_tile=1024, n1=200, n2=82,
                      dft_steps=(48, 14))
    y_sample = _trunk(x_sample, meta_tokens, p, tile=688, in_tiles=(1376, 1040), last_tile=1024, n1=16, n2=257)
    return (y_prompt, y_sample)
```
